```python
import jax, jax.numpy as jnp
from jax import lax
import numpy as np

D_MODEL = 1024
BATCH = 8
SEQ = 4096
DEPTH = 2

HEAD_DIM = 64
N_SB_HEADS = 12
SB_WIDTH = N_SB_HEADS * HEAD_DIM
N_MEM_HEADS = 4
MEM_WIDTH = N_MEM_HEADS * HEAD_DIM
MEM_TOKENS = 256
MIX_WIDTH = SB_WIDTH + MEM_WIDTH
POOL_WINDOWS = (2, 4, 8, 16)
N_POOL_GROUPS = len(POOL_WINDOWS)
POOL_WIDTH = SB_WIDTH
POOL_GROUP = POOL_WIDTH // N_POOL_GROUPS
D_FF = -(-8 * D_MODEL // (3 * 256)) * 256
SB_BLOCK = 128
N_A_LAYERS = DEPTH // 2
N_B_LAYERS = DEPTH - N_A_LAYERS
EPS = 1e-6

kernel_name = "yoco_pool_stickbreak_hybrid"


def rmsnorm(x, g):
    xf = x.astype(jnp.float32)
    y = xf * lax.rsqrt(jnp.mean(xf * xf, axis=-1, keepdims=True) + EPS)
    return (y * g.astype(jnp.float32)).astype(x.dtype)


def swiglu(h, w_gu, w_down):
    gate, up = jnp.split(h @ w_gu, 2, axis=-1)
    return (jax.nn.silu(gate) * up) @ w_down


def multiscale_pool(u):
    s = u.shape[1]
    uf = u.astype(jnp.float32)
    cs = jnp.cumsum(uf, axis=1)
    pos = jnp.arange(s)
    outs = []
    for g, w in enumerate(POOL_WINDOWS):
        c = cs[..., g * POOL_GROUP:(g + 1) * POOL_GROUP]
        prev = jnp.pad(c, ((0, 0), (w, 0), (0, 0)))[:, :s]
        cnt = jnp.minimum(pos + 1, w).astype(jnp.float32)[None, :, None]
        outs.append((c - prev) / cnt)
    pooled = jnp.concatenate(outs, axis=-1)
    return (pooled - uf).astype(u.dtype)


def memory_kv(mem, mem_norm, w_mem_kv):
    b, m, _ = mem.shape
    k, v = jnp.split(rmsnorm(mem, mem_norm) @ w_mem_kv, 2, axis=-1)
    return (k.reshape(b, m, N_MEM_HEADS, HEAD_DIM), v.reshape(b, m, N_MEM_HEADS, HEAD_DIM))


def memory_attention(q, mk, mv):
    b, s, _ = q.shape
    qh = q.reshape(b, s, N_MEM_HEADS, HEAD_DIM)
    logits = jnp.einsum("bshd,bmhd->bhsm", qh, mk).astype(jnp.float32) * (HEAD_DIM ** -0.5)
    p = jax.nn.softmax(logits, axis=-1).astype(mv.dtype)
    return jnp.einsum("bhsm,bmhd->bshd", p, mv).reshape(b, s, MEM_WIDTH)


def stick_breaking_attention(q, k, v):
    s = q.shape[2]
    scale = HEAD_DIM ** -0.5
    outs = []
    for i in range(s // SB_BLOCK):
        q0 = i * SB_BLOCK
        end = q0 + SB_BLOCK
        qb = q[:, :, q0:end]
        kb = k[:, :, :end]
        vb = v[:, :, :end]
        z = jnp.einsum("bhqd,bhkd->bhqk", qb, kb).astype(jnp.float32) * scale
        tpos = q0 + jnp.arange(SB_BLOCK)
        spos = jnp.arange(end)
        mask = spos[None, :] < tpos[:, None]
        log_not = jnp.where(mask, jax.nn.log_sigmoid(-z), 0.0)
        later = lax.cumsum(log_not, axis=3, reverse=True) - log_not
        wts = jnp.where(mask, jnp.exp(jax.nn.log_sigmoid(z) + later), 0.0)
        outs.append(jnp.einsum("bhqk,bhkd->bhqd", wts.astype(vb.dtype), vb))
    return jnp.concatenate(outs, axis=2)


def pool_layer(x, mem, mem_norm, norm_mix, w_in, w_group, scale, w_mem_kv, w_out, norm_ffn, w_gu, w_down):
    b, s, _ = x.shape
    proj = rmsnorm(x, norm_mix) @ w_in
    u_pool, q_mem = proj[..., :POOL_WIDTH], proj[..., POOL_WIDTH:]
    pooled = multiscale_pool(u_pool).reshape(b, s, N_POOL_GROUPS, POOL_GROUP)
    grouped = jnp.einsum("bsgc,gcd->bsgd", pooled, w_group).reshape(b, s, POOL_WIDTH) * scale
    mk, mv = memory_kv(mem, mem_norm, w_mem_kv)
    mem_out = memory_attention(q_mem, mk, mv)
    x = x + jnp.concatenate([grouped, mem_out], axis=-1) @ w_out
    return x + swiglu(rmsnorm(x, norm_ffn), w_gu, w_down)


def sb_layer(x, mem, k_sh, v_sh, mem_norm, norm_mix, w_q, w_mem_kv, w_out, norm_ffn, w_gu, w_down):
    b, s, _ = x.shape
    proj = rmsnorm(x, norm_mix) @ w_q
    q_sb = proj[..., :SB_WIDTH].reshape(b, s, N_SB_HEADS, HEAD_DIM).transpose(0, 2, 1, 3)
    q_mem = proj[..., SB_WIDTH:]
    sb_out = stick_breaking_attention(q_sb, k_sh, v_sh).transpose(0, 2, 1, 3).reshape(b, s, SB_WIDTH)
    mk, mv = memory_kv(mem, mem_norm, w_mem_kv)
    mem_out = memory_attention(q_mem, mk, mv)
    x = x + jnp.concatenate([sb_out, mem_out], axis=-1) @ w_out
    return x + swiglu(rmsnorm(x, norm_ffn), w_gu, w_down)


def setup_inputs(seed: int = 0) -> dict:
    key = jax.random.key(seed)
    ks = jax.random.split(key, 24)
    f32 = jnp.float32
    out_gain = (2.0 * DEPTH) ** -0.5

    def w(k, shape, fan_in, gain=1.0):
        return jax.random.normal(k, shape, f32) * (gain * fan_in ** -0.5)

    def g(k, shape):
        return 1.0 + 0.05 * jax.random.normal(k, shape, f32)

    na, nb = N_A_LAYERS, N_B_LAYERS
    return {
        "x": jax.random.normal(ks[0], (BATCH, SEQ, D_MODEL), f32),
        "mem": jax.random.normal(ks[1], (BATCH, MEM_TOKENS, D_MODEL), f32),
        "mem_norm": g(ks[2], (D_MODEL,)),
        "a_norm_mix": g(ks[3], (na, D_MODEL)),
        "a_w_in": w(ks[4], (na, D_MODEL, MIX_WIDTH), D_MODEL),
        "a_w_group": w(ks[5], (na, N_POOL_GROUPS, POOL_GROUP, POOL_GROUP), POOL_GROUP),
        "a_scale": g(ks[6], (na, POOL_WIDTH)),
        "a_w_mem_kv": w(ks[7], (na, D_MODEL, 2 * MEM_WIDTH), D_MODEL),
        "a_w_out": w(ks[8], (na, MIX_WIDTH, D_MODEL), MIX_WIDTH, out_gain),
        "a_norm_ffn": g(ks[9], (na, D_MODEL)),
        "a_w_gu": w(ks[10], (na, D_MODEL, 2 * D_FF), D_MODEL),
        "a_w_down": w(ks[11], (na, D_FF, D_MODEL), D_FF, out_gain),
        "kv_norm": g(ks[12], (D_MODEL,)),
        "w_kv": w(ks[13], (D_MODEL, 2 * SB_WIDTH), D_MODEL),
        "b_norm_mix": g(ks[14], (nb, D_MODEL)),
        "b_w_q": w(ks[15], (nb, D_MODEL, MIX_WIDTH), D_MODEL),
        "b_w_mem_kv": w(ks[16], (nb, D_MODEL, 2 * MEM_WIDTH), D_MODEL),
        "b_w_out": w(ks[17], (nb, MIX_WIDTH, D_MODEL), MIX_WIDTH, out_gain),
        "b_norm_ffn": g(ks[18], (nb, D_MODEL)),
        "b_w_gu": w(ks[19], (nb, D_MODEL, 2 * D_FF), D_MODEL),
        "b_w_down": w(ks[20], (nb, D_FF, D_MODEL), D_FF, out_gain),
        "final_norm": g(ks[21], (D_MODEL,)),
    }


def reference(x, mem, mem_norm, a_norm_mix, a_w_in, a_w_group, a_scale, a_w_mem_kv, a_w_out,
              a_norm_ffn, a_w_gu, a_w_down, kv_norm, w_kv, b_norm_mix, b_w_q, b_w_mem_kv,
              b_w_out, b_norm_ffn, b_w_gu, b_w_down, final_norm):
    b, s, _ = x.shape
    k_sh = v_sh = None
    for layer in range(DEPTH):
        if layer < N_A_LAYERS:
            i = layer
            x = pool_layer(x, mem, mem_norm, a_norm_mix[i], a_w_in[i], a_w_group[i], a_scale[i],
                           a_w_mem_kv[i], a_w_out[i], a_norm_ffn[i], a_w_gu[i], a_w_down[i])
        else:
            if layer == N_A_LAYERS:
                kv = rmsnorm(x, kv_norm) @ w_kv
                k_sh = kv[..., :SB_WIDTH].reshape(b, s, N_SB_HEADS, HEAD_DIM).transpose(0, 2, 1, 3)
                v_sh = kv[..., SB_WIDTH:].reshape(b, s, N_SB_HEADS, HEAD_DIM).transpose(0, 2, 1, 3)
            j = layer - N_A_LAYERS
            x = sb_layer(x, mem, k_sh, v_sh, mem_norm, b_norm_mix[j], b_w_q[j], b_w_mem_kv[j],
                         b_w_out[j], b_norm_ffn[j], b_w_gu[j], b_w_down[j])
    return rmsnorm(x, final_norm)
```

```python
import functools

import jax
import jax.numpy as jnp
from jax import lax
from jax.experimental import pallas as pl
from jax.experimental.pallas import tpu as pltpu

HEAD_DIM = 64
N_SB_HEADS = 12
SB_WIDTH = N_SB_HEADS * HEAD_DIM
N_MEM_HEADS = 4
MEM_WIDTH = N_MEM_HEADS * HEAD_DIM
MIX_WIDTH = SB_WIDTH + MEM_WIDTH
POOL_WINDOWS = (2, 4, 8, 16)
POOL_WIDTH = SB_WIDTH
POOL_GROUP = POOL_WIDTH // len(POOL_WINDOWS)
EPS = 1e-6
QK_SCALE = HEAD_DIM ** -0.5

LANES = 128
POOL_PAD = 32
FF_CHUNK = 256
VMEM_LIMIT = 56 * 1024 * 1024

F32 = jnp.float32
BF16 = jnp.bfloat16
NT_DIMS = (((1,), (1,)), ((), ()))


def _const_spec(shape):
    nd = len(shape)
    return pl.BlockSpec(shape, lambda *_: (0,) * nd, pipeline_mode=pl.Buffered(1))


def _rms_scale(x):
    return x * lax.rsqrt(jnp.mean(x * x, axis=-1, keepdims=True) + EPS)


def _split_heads(t, rows):
    lane = lax.broadcasted_iota(jnp.int32, (rows, LANES), 1)
    first = lane < HEAD_DIM
    zero = jnp.zeros_like(t)
    return jnp.concatenate([jnp.where(first, t, zero), jnp.where(first, zero, t)], axis=0)


def _memkv_kernel(mem_ref, g_ref, w_ref, k2_ref, v2_ref, *, m):
    h = (_rms_scale(mem_ref[...]) * g_ref[...]).astype(BF16)
    kv = jnp.dot(h, w_ref[...], preferred_element_type=F32)
    for p in range(MEM_WIDTH // LANES):
        k2_ref[p] = _split_heads(kv[:, p * LANES:(p + 1) * LANES], m).astype(BF16)
        v2_ref[p] = _split_heads(kv[:, MEM_WIDTH + p * LANES:MEM_WIDTH + (p + 1) * LANES], m).astype(BF16)


def _memory_kv(mem, mem_norm, w_mem_kv):
    b, m, d = mem.shape
    n_layers = w_mem_kv.shape[0]
    pairs = MEM_WIDTH // LANES
    out = jax.ShapeDtypeStruct((n_layers, b, pairs, 2 * m, LANES), BF16)
    out_spec = pl.BlockSpec((None, None, pairs, 2 * m, LANES), lambda l, i: (l, i, 0, 0, 0))
    return pl.pallas_call(
        functools.partial(_memkv_kernel, m=m),
        out_shape=(out, out),
        grid=(n_layers, b),
        in_specs=[
            pl.BlockSpec((None, m, d), lambda l, i: (i, 0, 0)),
            pl.BlockSpec((1, d), lambda l, i: (0, 0)),
            pl.BlockSpec((None, d, 2 * MEM_WIDTH), lambda l, i: (l, 0, 0)),
        ],
        out_specs=(out_spec, out_spec),
        name="memory_kv",
    )(mem, mem_norm.reshape(1, d), w_mem_kv)


def _memory_attention(qm, k2_ref, v2_ref, m):
    t = qm.shape[0]
    lane = lax.broadcasted_iota(jnp.int32, (t, LANES), 1)
    outs = []
    for p in range(MEM_WIDTH // LANES):
        logits = lax.dot_general(qm[:, p * LANES:(p + 1) * LANES], k2_ref[p], NT_DIMS,
                                 preferred_element_type=F32)
        probs, inv = [], []
        for hh in range(2):
            lg = logits[:, hh * m:(hh + 1) * m]
            pe = jnp.exp(lg - jnp.max(lg, axis=1, keepdims=True))
            probs.append(pe)
            inv.append(1.0 / jnp.sum(pe, axis=1, keepdims=True))
        pv = jnp.dot(jnp.concatenate(probs, axis=1).astype(BF16), v2_ref[p],
                     preferred_element_type=F32)
        outs.append(pv * jnp.where(lane < HEAD_DIM, inv[0], inv[1]))
    return jnp.concatenate(outs, axis=1)


def _pool_mixer_kernel(x_ref, g_ref, win_ref, wgrp_ref, scale_ref, k2_ref, v2_ref, wout_ref, o_ref,
                       e_ref, s2_ref, s4_ref, s8_ref, *, tm, m):
    t = pl.program_id(1)
    x = x_ref[...]
    h = (_rms_scale(x) * g_ref[...]).astype(BF16)
    proj = jnp.dot(h, win_ref[...], preferred_element_type=F32)
    u = proj[:, :POOL_WIDTH]

    @pl.when(t == 0)
    def _():
        e_ref[0:POOL_PAD, :] = jnp.zeros((POOL_PAD, POOL_WIDTH), F32)

    @pl.when(t > 0)
    def _():
        e_ref[POOL_PAD - 16:POOL_PAD, :] = e_ref[tm + POOL_PAD - 16:tm + POOL_PAD, :]

    e_ref[POOL_PAD:, :] = u
    n = tm + POOL_PAD
    s2_ref[8:, :] = e_ref[8:, :] + e_ref[7:n - 1, :]
    s4_ref[16:, 128:] = s2_ref[16:, 128:] + s2_ref[14:n - 2, 128:]
    s8_ref[24:, 384:] = s4_ref[24:, 384:] + s4_ref[20:n - 4, 384:]
    s16 = s8_ref[32:, 512:] + s8_ref[24:n - 8, 512:]

    pos1 = t * tm + lax.broadcasted_iota(jnp.int32, (tm, 1), 0) + 1
    inv = [1.0 / jnp.minimum(pos1, w).astype(F32) for w in POOL_WINDOWS]
    lane = lax.broadcasted_iota(jnp.int32, (tm, LANES), 1)
    low = lane < (POOL_GROUP - LANES)
    pooled = jnp.concatenate([
        s2_ref[POOL_PAD:, 0:128] * inv[0],
        jnp.where(low, s2_ref[POOL_PAD:, 128:256] * inv[0], s4_ref[POOL_PAD:, 128:256] * inv[1]),
        s4_ref[POOL_PAD:, 256:384] * inv[1],
        s8_ref[POOL_PAD:, 384:512] * inv[2],
        jnp.where(low, s8_ref[POOL_PAD:, 512:640] * inv[2], s16[:, 0:128] * inv[3]),
        s16[:, 128:256] * inv[3],
    ], axis=1) - u

    grouped = jnp.dot(pooled.astype(BF16), wgrp_ref[...], preferred_element_type=F32) * scale_ref[...]
    qm = (proj[:, POOL_WIDTH:] * QK_SCALE).astype(BF16)
    mem_out = _memory_attention(qm, k2_ref, v2_ref, m)
    o_ref[...] = (x
                  + jnp.dot(grouped.astype(BF16), wout_ref[0:POOL_WIDTH, :], preferred_element_type=F32)
                  + jnp.dot(mem_out.astype(BF16), wout_ref[POOL_WIDTH:, :], preferred_element_type=F32))


def _pool_mixer(x, g, w_in, w_group_full, scale, k2, v2, layer, w_out, tm):
    b, s, d = x.shape
    m = k2.shape[3] // 2
    pairs = MEM_WIDTH // LANES
    kv_spec = pl.BlockSpec((None, None, pairs, 2 * m, LANES), lambda i, t: (layer, i, 0, 0, 0))
    return pl.pallas_call(
        functools.partial(_pool_mixer_kernel, tm=tm, m=m),
        out_shape=jax.ShapeDtypeStruct((b, s, d), F32),
        grid=(b, s // tm),
        in_specs=[
            pl.BlockSpec((None, tm, d), lambda i, t: (i, t, 0)),
            _const_spec((1, d)),
            _const_spec((d, MIX_WIDTH)),
            _const_spec((POOL_WIDTH, POOL_WIDTH)),
            _const_spec((1, POOL_WIDTH)),
            kv_spec, kv_spec,
            _const_spec((MIX_WIDTH, d)),
        ],
        out_specs=pl.BlockSpec((None, tm, d), lambda i, t: (i, t, 0)),
        scratch_shapes=[pltpu.VMEM((tm + POOL_PAD, POOL_WIDTH), F32)] * 4,
        compiler_params=pltpu.CompilerParams(
            dimension_semantics=("arbitrary", "arbitrary"), vmem_limit_bytes=VMEM_LIMIT),
        name="pool_mixer",
    )(x, g.reshape(1, d), w_in, w_group_full, scale.reshape(1, POOL_WIDTH), k2, v2, w_out)


def _ffn_kernel(x_ref, g_ref, wg_ref, wu_ref, wd_ref, gf_ref, o_ref, h_ref, acc_ref, *, final_norm):
    x = x_ref[...]
    h_ref[...] = (_rms_scale(x) * g_ref[...]).astype(BF16)
    acc_ref[...] = x

    def chunk(c, carry):
        h = h_ref[...]
        gate = jnp.dot(h, wg_ref[c], preferred_element_type=F32)
        up = jnp.dot(h, wu_ref[c], preferred_element_type=F32)
        act = gate * (1.0 / (1.0 + jnp.exp(-gate))) * up
        acc_ref[...] += jnp.dot(act.astype(BF16), wd_ref[c], preferred_element_type=F32)
        return carry

    lax.fori_loop(0, wg_ref.shape[0], chunk, 0)
    y = acc_ref[...]
    if final_norm:
        y = _rms_scale(y) * gf_ref[...]
    o_ref[...] = y


def _ffn(x2d, g, w_gate, w_up, w_down, g_final, tm, final_norm):
    n, d = x2d.shape
    nc = w_gate.shape[0]
    return pl.pallas_call(
        functools.partial(_ffn_kernel, final_norm=final_norm),
        out_shape=jax.ShapeDtypeStruct((n, d), F32),
        grid=(n // tm,),
        in_specs=[
            pl.BlockSpec((tm, d), lambda i: (i, 0)),
            _const_spec((1, d)),
            _const_spec((nc, d, FF_CHUNK)),
            _const_spec((nc, d, FF_CHUNK)),
            _const_spec((nc, FF_CHUNK, d)),
            _const_spec((1, d)),
        ],
        out_specs=pl.BlockSpec((tm, d), lambda i: (i, 0)),
        scratch_shapes=[pltpu.VMEM((tm, d), BF16), pltpu.VMEM((tm, d), F32)],
        compiler_params=pltpu.CompilerParams(
            dimension_semantics=("arbitrary",), vmem_limit_bytes=VMEM_LIMIT),
        name="ffn_final" if final_norm else "ffn",
    )(x2d, g.reshape(1, d), w_gate, w_up, w_down, g_final.reshape(1, d))


def _qkv_kernel(x_ref, gkv_ref, gq_ref, wkv_ref, wq_ref, kv_ref, q_ref):
    xn = _rms_scale(x_ref[...])
    kv = jnp.dot((xn * gkv_ref[...]).astype(BF16), wkv_ref[...], preferred_element_type=F32)
    kv_ref[...] = kv.astype(BF16)
    q = jnp.dot((xn * gq_ref[...]).astype(BF16), wq_ref[...], preferred_element_type=F32)
    q_ref[...] = (q * QK_SCALE).astype(BF16)


def _q_kernel(x_ref, gq_ref, wq_ref, q_ref):
    h = (_rms_scale(x_ref[...]) * gq_ref[...]).astype(BF16)
    q_ref[...] = (jnp.dot(h, wq_ref[...], preferred_element_type=F32) * QK_SCALE).astype(BF16)


def _qkv_proj(x2d, g_kv, w_kv, g_q, w_q, tm):
    n, d = x2d.shape
    row = lambda w: pl.BlockSpec((tm, w), lambda i: (i, 0))
    params = pltpu.CompilerParams(dimension_semantics=("arbitrary",), vmem_limit_bytes=VMEM_LIMIT)
    if w_kv is None:
        return None, pl.pallas_call(
            _q_kernel, out_shape=jax.ShapeDtypeStruct((n, MIX_WIDTH), BF16), grid=(n // tm,),
            in_specs=[row(d), _const_spec((1, d)), _const_spec((d, MIX_WIDTH))],
            out_specs=row(MIX_WIDTH), compiler_params=params, name="q_proj",
        )(x2d, g_q.reshape(1, d), w_q)
    return pl.pallas_call(
        _qkv_kernel,
        out_shape=(jax.ShapeDtypeStruct((n, 2 * SB_WIDTH), BF16), jax.ShapeDtypeStruct((n, MIX_WIDTH), BF16)),
        grid=(n // tm,),
        in_specs=[row(d), _const_spec((1, d)), _const_spec((1, d)),
                  _const_spec((d, 2 * SB_WIDTH)), _const_spec((d, MIX_WIDTH))],
        out_specs=(row(2 * SB_WIDTH), row(MIX_WIDTH)),
        compiler_params=params, name="qkv_proj",
    )(x2d, g_kv.reshape(1, d), g_q.reshape(1, d), w_kv, w_q)


def _sb_kernel(q_ref, k_ref, v_ref, tri_ref, o_ref, *, tq, tk):
    qi = pl.program_id(2)
    q = q_ref[...]
    tri = tri_ref[...]
    diag_blocks = tq // tk

    def block(jb, carry, masked):
        acc, c0, c1 = carry
        start = pl.multiple_of(jb * tk, tk)
        k2 = _split_heads(k_ref[pl.ds(start, tk), :], tk)
        v2 = _split_heads(v_ref[pl.ds(start, tk), :], tk)
        z = lax.dot_general(q, k2, NT_DIMS, preferred_element_type=F32)
        softplus = jnp.maximum(z, 0.0) + jnp.log(1.0 + jnp.exp(-jnp.abs(z)))
        ln = -softplus
        if masked:
            qpos = qi * tq + lax.broadcasted_iota(jnp.int32, (tq, 2 * tk), 0)
            col = lax.broadcasted_iota(jnp.int32, (tq, 2 * tk), 1)
            kpos = jb * tk + jnp.where(col < tk, col, col - tk)
            keep = kpos < qpos
            ln = jnp.where(keep, ln, 0.0)
        hi = ln.astype(BF16)
        lo = (ln - hi.astype(F32)).astype(BF16)
        later = (jnp.dot(hi, tri, preferred_element_type=F32)
                 + jnp.dot(lo, tri, preferred_element_type=F32))
        cb = jnp.concatenate([jnp.broadcast_to(c0, (tq, tk)), jnp.broadcast_to(c1, (tq, tk))], axis=1)
        w = jnp.exp(z - softplus + later + cb)
        if masked:
            w = jnp.where(keep, w, 0.0)
        acc = acc + jnp.dot(w.astype(BF16), v2, preferred_element_type=F32)
        c0 = c0 + jnp.sum(ln[:, :tk], axis=1, keepdims=True)
        c1 = c1 + jnp.sum(ln[:, tk:], axis=1, keepdims=True)
        return acc, c0, c1

    carry = (jnp.zeros((tq, LANES), F32), jnp.zeros((tq, 1), F32), jnp.zeros((tq, 1), F32))
    first = qi * diag_blocks
    for d in range(diag_blocks - 1, -1, -1):
        carry = block(first + d, carry, True)
    carry = lax.fori_loop(0, first, lambda i, c: block(first - 1 - i, c, False), carry)
    o_ref[...] = carry[0].astype(o_ref.dtype)


def _stick_breaking(q, kv, tq, tk):
    b, s, _ = q.shape
    pairs = SB_WIDTH // LANES
    r = lax.broadcasted_iota(jnp.int32, (2 * tk, 2 * tk), 0)
    c = lax.broadcasted_iota(jnp.int32, (2 * tk, 2 * tk), 1)
    tri = ((r > c) & ((r < tk) == (c < tk))).astype(BF16)
    return pl.pallas_call(
        functools.partial(_sb_kernel, tq=tq, tk=tk),
        out_shape=jax.ShapeDtypeStruct((b, s, SB_WIDTH), BF16),
        grid=(b, pairs, s // tq),
        in_specs=[
            pl.BlockSpec((None, tq, LANES), lambda i, p, t: (i, t, p)),
            pl.BlockSpec((None, s, LANES), lambda i, p, t: (i, 0, p)),
            pl.BlockSpec((None, s, LANES), lambda i, p, t: (i, 0, pairs + p)),
            _const_spec((2 * tk, 2 * tk)),
        ],
        out_specs=pl.BlockSpec((None, tq, LANES), lambda i, p, t: (i, t, p)),
        compiler_params=pltpu.CompilerParams(
            dimension_semantics=("arbitrary", "arbitrary", "arbitrary"), vmem_limit_bytes=VMEM_LIMIT),
        name="stick_breaking",
    )(q, kv, kv, tri)


def _sb_mixer_kernel(x_ref, sb_ref, qm_ref, k2_ref, v2_ref, wout_ref, o_ref, *, m):
    mem_out = _memory_attention(qm_ref[...], k2_ref, v2_ref, m)
    o_ref[...] = (x_ref[...]
                  + jnp.dot(sb_ref[...], wout_ref[0:SB_WIDTH, :], preferred_element_type=F32)
                  + jnp.dot(mem_out.astype(BF16), wout_ref[SB_WIDTH:, :], preferred_element_type=F32))


def _sb_mixer(x, sb_out, q, k2, v2, layer, w_out, tm):
    b, s, d = x.shape
    m = k2.shape[3] // 2
    pairs = MEM_WIDTH // LANES
    kv_spec = pl.BlockSpec((None, None, pairs, 2 * m, LANES), lambda i, t: (layer, i, 0, 0, 0))
    return pl.pallas_call(
        functools.partial(_sb_mixer_kernel, m=m),
        out_shape=jax.ShapeDtypeStruct((b, s, d), F32),
        grid=(b, s // tm),
        in_specs=[
            pl.BlockSpec((None, tm, d), lambda i, t: (i, t, 0)),
            pl.BlockSpec((None, tm, SB_WIDTH), lambda i, t: (i, t, 0)),
            pl.BlockSpec((None, tm, MEM_WIDTH), lambda i, t: (i, t, SB_WIDTH // MEM_WIDTH)),
            kv_spec, kv_spec,
            _const_spec((MIX_WIDTH, d)),
        ],
        out_specs=pl.BlockSpec((None, tm, d), lambda i, t: (i, t, 0)),
        compiler_params=pltpu.CompilerParams(
            dimension_semantics=("arbitrary", "arbitrary"), vmem_limit_bytes=VMEM_LIMIT),
        name="sb_mixer",
    )(x, sb_out, q, k2, v2, w_out)


def _block_diag(w_group):
    g, c, _ = w_group.shape
    eye = jnp.eye(g, dtype=w_group.dtype)
    return jnp.einsum("gcd,gh->gchd", w_group, eye).reshape(g * c, g * c)


def _ffn_weights(w_gu, w_down):
    d, two_f = w_gu.shape
    f = two_f // 2
    nc = f // FF_CHUNK
    gu = w_gu.astype(BF16).reshape(d, 2, nc, FF_CHUNK).transpose(1, 2, 0, 3)
    return gu[0], gu[1], w_down.astype(BF16).reshape(nc, FF_CHUNK, d)


def kernel(x, mem, mem_norm, a_norm_mix, a_w_in, a_w_group, a_scale, a_w_mem_kv, a_w_out, a_norm_ffn,
           a_w_gu, a_w_down, kv_norm, w_kv, b_norm_mix, b_w_q, b_w_mem_kv, b_w_out, b_norm_ffn, b_w_gu,
           b_w_down, final_norm):
    b, s, d = x.shape
    na, nb = a_w_in.shape[0], b_w_q.shape[0]
    tm = min(512, s)
    tq, tk = min(256, s), min(128, s)

    k2, v2 = _memory_kv(mem, mem_norm, jnp.concatenate([a_w_mem_kv, b_w_mem_kv], axis=0).astype(BF16))

    for i in range(na):
        x = _pool_mixer(x, a_norm_mix[i], a_w_in[i].astype(BF16), _block_diag(a_w_group[i]).astype(BF16),
                        a_scale[i], k2, v2, i, a_w_out[i].astype(BF16), tm)
        wg, wu, wd = _ffn_weights(a_w_gu[i], a_w_down[i])
        last = nb == 0 and i == na - 1
        x = _ffn(x.reshape(b * s, d), a_norm_ffn[i], wg, wu, wd, final_norm, tm, last).reshape(b, s, d)

    kv = None
    for j in range(nb):
        kv_j, q = _qkv_proj(x.reshape(b * s, d), kv_norm, w_kv.astype(BF16) if j == 0 else None,
                            b_norm_mix[j], b_w_q[j].astype(BF16), tm)
        if j == 0:
            kv = kv_j.reshape(b, s, 2 * SB_WIDTH)
        q = q.reshape(b, s, MIX_WIDTH)
        sb_out = _stick_breaking(q, kv, tq, tk)
        x = _sb_mixer(x, sb_out, q, k2, v2, na + j, b_w_out[j].astype(BF16), tm)
        wg, wu, wd = _ffn_weights(b_w_gu[j], b_w_down[j])
        x = _ffn(x.reshape(b * s, d), b_norm_ffn[j], wg, wu, wd, final_norm, tm, j == nb - 1).reshape(b, s, d)
    return x
```

```python
import functools

import jax
import jax.numpy as jnp
from jax import lax
from jax.experimental import pallas as pl
from jax.experimental.pallas import tpu as pltpu

HEAD_DIM = 64
N_SB_HEADS = 12
SB_WIDTH = N_SB_HEADS * HEAD_DIM
N_MEM_HEADS = 4
MEM_WIDTH = N_MEM_HEADS * HEAD_DIM
MIX_WIDTH = SB_WIDTH + MEM_WIDTH
POOL_WINDOWS = (2, 4, 8, 16)
POOL_WIDTH = SB_WIDTH
POOL_GROUP = POOL_WIDTH // len(POOL_WINDOWS)
EPS = 1e-6
LOG2_E = 1.4426950408889634
QK_SCALE_LOG2 = HEAD_DIM ** -0.5 * LOG2_E

LANES = 128
POOL_PAD = 32
FF_CHUNK = 256
SB_TILE = 128
SB_WINDOW = 3
SB_UNROLL = 8
SKIP_LOG2 = 151.0
VMEM_LIMIT = 56 * 1024 * 1024

F32 = jnp.float32
BF16 = jnp.bfloat16
NT_DIMS = (((1,), (1,)), ((), ()))


def _const_spec(shape):
    nd = len(shape)
    return pl.BlockSpec(shape, lambda *_: (0,) * nd, pipeline_mode=pl.Buffered(1))


def _rms_scale(x):
    return x * lax.rsqrt(jnp.mean(x * x, axis=-1, keepdims=True) + EPS)


def _split_heads(t, rows):
    lane = lax.broadcasted_iota(jnp.int32, (rows, LANES), 1)
    first = lane < HEAD_DIM
    zero = jnp.zeros_like(t)
    return jnp.concatenate([jnp.where(first, t, zero), jnp.where(first, zero, t)], axis=0)


def _memkv_kernel(mem_ref, g_ref, w_ref, k2_ref, v2_ref, *, m):
    h = (_rms_scale(mem_ref[...]) * g_ref[...]).astype(BF16)
    kv = jnp.dot(h, w_ref[...], preferred_element_type=F32)
    for p in range(MEM_WIDTH // LANES):
        k2_ref[p] = _split_heads(kv[:, p * LANES:(p + 1) * LANES], m).astype(BF16)
        v2_ref[p] = _split_heads(kv[:, MEM_WIDTH + p * LANES:MEM_WIDTH + (p + 1) * LANES], m).astype(BF16)


def _memory_kv(mem, mem_norm, w_mem_kv):
    b, m, d = mem.shape
    n_layers = w_mem_kv.shape[0]
    pairs = MEM_WIDTH // LANES
    out = jax.ShapeDtypeStruct((n_layers, b, pairs, 2 * m, LANES), BF16)
    out_spec = pl.BlockSpec((None, None, pairs, 2 * m, LANES), lambda l, i: (l, i, 0, 0, 0))
    return pl.pallas_call(
        functools.partial(_memkv_kernel, m=m),
        out_shape=(out, out),
        grid=(n_layers, b),
        in_specs=[
            pl.BlockSpec((None, m, d), lambda l, i: (i, 0, 0)),
            pl.BlockSpec((1, d), lambda l, i: (0, 0)),
            pl.BlockSpec((None, d, 2 * MEM_WIDTH), lambda l, i: (l, 0, 0)),
        ],
        out_specs=(out_spec, out_spec),
        name="memory_kv",
    )(mem, mem_norm.reshape(1, d), w_mem_kv)


def _memory_attention(qm, k2_ref, v2_ref, m):
    t = qm.shape[0]
    lane = lax.broadcasted_iota(jnp.int32, (t, LANES), 1)
    outs = []
    for p in range(MEM_WIDTH // LANES):
        logits = lax.dot_general(qm[:, p * LANES:(p + 1) * LANES], k2_ref[p], NT_DIMS,
                                 preferred_element_type=F32)
        probs, inv = [], []
        for hh in range(2):
            lg = logits[:, hh * m:(hh + 1) * m]
            pe = jnp.exp2(lg - jnp.max(lg, axis=1, keepdims=True))
            probs.append(pe)
            inv.append(1.0 / jnp.sum(pe, axis=1, keepdims=True))
        pv = jnp.dot(jnp.concatenate(probs, axis=1).astype(BF16), v2_ref[p],
                     preferred_element_type=F32)
        outs.append(pv * jnp.where(lane < HEAD_DIM, inv[0], inv[1]))
    return jnp.concatenate(outs, axis=1)


def _pool_mixer_kernel(x_ref, g_ref, win_ref, wgrp_ref, scale_ref, k2_ref, v2_ref, wout_ref, o_ref,
                       e_ref, s2_ref, s4_ref, s8_ref, *, tm, m):
    t = pl.program_id(1)
    x = x_ref[...]
    h = (_rms_scale(x) * g_ref[...]).astype(BF16)
    proj = jnp.dot(h, win_ref[...], preferred_element_type=F32)
    u = proj[:, :POOL_WIDTH]

    @pl.when(t == 0)
    def _():
        e_ref[0:POOL_PAD, :] = jnp.zeros((POOL_PAD, POOL_WIDTH), F32)

    @pl.when(t > 0)
    def _():
        e_ref[POOL_PAD - 16:POOL_PAD, :] = e_ref[tm + POOL_PAD - 16:tm + POOL_PAD, :]

    e_ref[POOL_PAD:, :] = u
    n = tm + POOL_PAD
    s2_ref[8:, :] = e_ref[8:, :] + e_ref[7:n - 1, :]
    s4_ref[16:, 128:] = s2_ref[16:, 128:] + s2_ref[14:n - 2, 128:]
    s8_ref[24:, 384:] = s4_ref[24:, 384:] + s4_ref[20:n - 4, 384:]
    s16 = s8_ref[32:, 512:] + s8_ref[24:n - 8, 512:]

    pos1 = t * tm + lax.broadcasted_iota(jnp.int32, (tm, 1), 0) + 1
    inv = [1.0 / jnp.minimum(pos1, w).astype(F32) for w in POOL_WINDOWS]
    lane = lax.broadcasted_iota(jnp.int32, (tm, LANES), 1)
    low = lane < (POOL_GROUP - LANES)
    pooled = jnp.concatenate([
        s2_ref[POOL_PAD:, 0:128] * inv[0],
        jnp.where(low, s2_ref[POOL_PAD:, 128:256] * inv[0], s4_ref[POOL_PAD:, 128:256] * inv[1]),
        s4_ref[POOL_PAD:, 256:384] * inv[1],
        s8_ref[POOL_PAD:, 384:512] * inv[2],
        jnp.where(low, s8_ref[POOL_PAD:, 512:640] * inv[2], s16[:, 0:128] * inv[3]),
        s16[:, 128:256] * inv[3],
    ], axis=1) - u

    grouped = jnp.dot(pooled.astype(BF16), wgrp_ref[...], preferred_element_type=F32) * scale_ref[...]
    qm = (proj[:, POOL_WIDTH:] * QK_SCALE_LOG2).astype(BF16)
    mem_out = _memory_attention(qm, k2_ref, v2_ref, m)
    o_ref[...] = (x
                  + jnp.dot(grouped.astype(BF16), wout_ref[0:POOL_WIDTH, :], preferred_element_type=F32)
                  + jnp.dot(mem_out.astype(BF16), wout_ref[POOL_WIDTH:, :], preferred_element_type=F32))


def _pool_mixer(x, g, w_in, w_group_full, scale, k2, v2, layer, w_out, tm):
    b, s, d = x.shape
    m = k2.shape[3] // 2
    pairs = MEM_WIDTH // LANES
    kv_spec = pl.BlockSpec((None, None, pairs, 2 * m, LANES), lambda i, t: (layer, i, 0, 0, 0))
    return pl.pallas_call(
        functools.partial(_pool_mixer_kernel, tm=tm, m=m),
        out_shape=jax.ShapeDtypeStruct((b, s, d), F32),
        grid=(b, s // tm),
        in_specs=[
            pl.BlockSpec((None, tm, d), lambda i, t: (i, t, 0)),
            _const_spec((1, d)),
            _const_spec((d, MIX_WIDTH)),
            _const_spec((POOL_WIDTH, POOL_WIDTH)),
            _const_spec((1, POOL_WIDTH)),
            kv_spec, kv_spec,
            _const_spec((MIX_WIDTH, d)),
        ],
        out_specs=pl.BlockSpec((None, tm, d), lambda i, t: (i, t, 0)),
        scratch_shapes=[pltpu.VMEM((tm + POOL_PAD, POOL_WIDTH), F32)] * 4,
        compiler_params=pltpu.CompilerParams(
            dimension_semantics=("arbitrary", "arbitrary"), vmem_limit_bytes=VMEM_LIMIT),
        name="pool_mixer",
    )(x, g.reshape(1, d), w_in, w_group_full, scale.reshape(1, POOL_WIDTH), k2, v2, w_out)


def _ffn_kernel(x_ref, g_ref, wgu_ref, wd_ref, gf_ref, o_ref, h_ref, *, final_norm):
    x = x_ref[...]
    h_ref[...] = (_rms_scale(x) * g_ref[...]).astype(BF16)
    d_ff = wd_ref.shape[0]
    for c in range(d_ff // FF_CHUNK):
        lo, hi = c * FF_CHUNK, (c + 1) * FF_CHUNK
        h = h_ref[...]
        gate = jnp.dot(h, wgu_ref[:, lo:hi], preferred_element_type=F32)
        up = jnp.dot(h, wgu_ref[:, d_ff + lo:d_ff + hi], preferred_element_type=F32)
        act = gate * (1.0 / (1.0 + jnp.exp(-gate))) * up
        down = jnp.dot(act.astype(BF16), wd_ref[lo:hi, :], preferred_element_type=F32)
        o_ref[...] = (x_ref[...] if c == 0 else o_ref[...]) + down
    if final_norm:
        o_ref[...] = _rms_scale(o_ref[...]) * gf_ref[...]


def _ffn(x2d, g, w_gu, w_down, g_final, tm, final_norm):
    n, d = x2d.shape
    d_ff = w_down.shape[0]
    return pl.pallas_call(
        functools.partial(_ffn_kernel, final_norm=final_norm),
        out_shape=jax.ShapeDtypeStruct((n, d), F32),
        grid=(n // tm,),
        in_specs=[
            pl.BlockSpec((tm, d), lambda i: (i, 0)),
            _const_spec((1, d)),
            _const_spec((d, 2 * d_ff)),
            _const_spec((d_ff, d)),
            _const_spec((1, d)),
        ],
        out_specs=pl.BlockSpec((tm, d), lambda i: (i, 0)),
        scratch_shapes=[pltpu.VMEM((tm, d), BF16)],
        compiler_params=pltpu.CompilerParams(
            dimension_semantics=("arbitrary",), vmem_limit_bytes=VMEM_LIMIT),
        name="ffn_final" if final_norm else "ffn",
    )(x2d, g.reshape(1, d), w_gu, w_down, g_final.reshape(1, d))


def _qkv_kernel(x_ref, gkv_ref, gq_ref, wkv_ref, wq_ref, kv_ref, q_ref):
    xn = _rms_scale(x_ref[...])
    kv = jnp.dot((xn * gkv_ref[...]).astype(BF16), wkv_ref[...], preferred_element_type=F32)
    kv_ref[...] = kv.astype(BF16)
    q = jnp.dot((xn * gq_ref[...]).astype(BF16), wq_ref[...], preferred_element_type=F32)
    q_ref[...] = (q * QK_SCALE_LOG2).astype(BF16)


def _q_kernel(x_ref, gq_ref, wq_ref, q_ref):
    h = (_rms_scale(x_ref[...]) * gq_ref[...]).astype(BF16)
    q_ref[...] = (jnp.dot(h, wq_ref[...], preferred_element_type=F32) * QK_SCALE_LOG2).astype(BF16)


def _qkv_proj(x2d, g_kv, w_kv, g_q, w_q, tm):
    n, d = x2d.shape
    row = lambda w: pl.BlockSpec((tm, w), lambda i: (i, 0))
    params = pltpu.CompilerParams(dimension_semantics=("arbitrary",), vmem_limit_bytes=VMEM_LIMIT)
    if w_kv is None:
        return None, pl.pallas_call(
            _q_kernel, out_shape=jax.ShapeDtypeStruct((n, MIX_WIDTH), BF16), grid=(n // tm,),
            in_specs=[row(d), _const_spec((1, d)), _const_spec((d, MIX_WIDTH))],
            out_specs=row(MIX_WIDTH), compiler_params=params, name="q_proj",
        )(x2d, g_q.reshape(1, d), w_q)
    return pl.pallas_call(
        _qkv_kernel,
        out_shape=(jax.ShapeDtypeStruct((n, 2 * SB_WIDTH), BF16), jax.ShapeDtypeStruct((n, MIX_WIDTH), BF16)),
        grid=(n // tm,),
        in_specs=[row(d), _const_spec((1, d)), _const_spec((1, d)),
                  _const_spec((d, 2 * SB_WIDTH)), _const_spec((d, MIX_WIDTH))],
        out_specs=(row(2 * SB_WIDTH), row(MIX_WIDTH)),
        compiler_params=params, name="qkv_proj",
    )(x2d, g_kv.reshape(1, d), g_q.reshape(1, d), w_kv, w_q)


def _sb_kernel(q_ref, k_ref, v_ref, tri_ref, o_ref, qpad_ref, acc_ref, c0_ref, c1_ref, *, tb, win):
    s = q_ref.shape[0]
    nblk = s // tb
    pad = (win - 1) * tb
    tri = tri_ref[...]
    rows = lax.broadcasted_iota(jnp.int32, (tb, 2 * tb), 0)
    cols = lax.broadcasted_iota(jnp.int32, (tb, 2 * tb), 1)
    before = jnp.where(cols < tb, cols, cols - tb) < rows

    def softplus2(z):
        neg_abs = lax.bitcast_convert_type(lax.bitcast_convert_type(z, jnp.int32) | jnp.int32(-2 ** 31), F32)
        return jnp.maximum(z, 0.0) + jnp.log(1.0 + jnp.exp2(neg_abs)) * LOG2_E

    def carry_lanes(c0, c1):
        return jnp.concatenate([jnp.broadcast_to(c0, (tb, tb)), jnp.broadcast_to(c1, (tb, tb))], axis=1)

    def row_sums(sp):
        return jnp.sum(sp[:, :tb], axis=1, keepdims=True), jnp.sum(sp[:, tb:], axis=1, keepdims=True)

    def key_block(j):
        start = pl.multiple_of(j * tb, tb)
        return (_split_heads(k_ref[pl.ds(start, tb), :], tb),
                _split_heads(v_ref[pl.ds(start, tb), :], tb))

    qpad_ref[0:s, :] = q_ref[...]
    qpad_ref[s:, :] = jnp.zeros((pad, LANES), q_ref.dtype)

    def sweep(i, carry):
        state, worst = carry
        j = nblk - 1 - i
        row0 = pl.multiple_of(j * tb, tb)
        k2, v2 = key_block(j)
        q = qpad_ref[pl.ds(row0, win * tb), :]
        z = lax.dot_general(q, k2, NT_DIMS, preferred_element_type=F32)
        sps, log_betas = [], []
        for d in range(win):
            zd = z[d * tb:(d + 1) * tb]
            sp = softplus2(zd)
            log_betas.append(zd - sp)
            sps.append(jnp.where(before, sp, 0.0) if d == 0 else sp)
        later = jnp.dot(jnp.concatenate(sps, axis=0).astype(BF16), tri, preferred_element_type=F32)
        ws = []
        for d in range(win):
            logw = log_betas[d] + later[d * tb:(d + 1) * tb]
            if d == 0:
                ws.append(jnp.where(before, jnp.exp2(logw), 0.0))
            else:
                ws.append(jnp.exp2(logw + carry_lanes(state[d - 1][0], state[d - 1][1])))
        pv = jnp.dot(jnp.concatenate(ws, axis=0).astype(BF16), v2, preferred_element_type=F32)
        new = []
        for d in range(win):
            s0, s1 = row_sums(sps[d])
            pvd = pv[d * tb:(d + 1) * tb]
            if d == 0:
                new.append((-s0, -s1, pvd))
            else:
                c0, c1, acc = state[d - 1]
                new.append((c0 - s0, c1 - s1, acc + pvd))
        out0 = pl.multiple_of(row0 + pad, tb)
        c0, c1, acc = new[win - 1]
        acc_ref[pl.ds(out0, tb), :] = acc
        c0_ref[pl.ds(out0, tb), :] = c0
        c1_ref[pl.ds(out0, tb), :] = c1
        worst = jnp.where(i >= win - 1, jnp.maximum(worst, jnp.maximum(c0, c1)), worst)
        return tuple(new[:win - 1]), worst

    zero = jnp.zeros((tb, 1), F32)
    init = (((zero, zero, jnp.zeros((tb, LANES), F32)),) * (win - 1), jnp.full((tb, 1), -SKIP_LOG2, F32))
    state, worst = lax.fori_loop(0, nblk, sweep, init, unroll=SB_UNROLL if nblk % SB_UNROLL == 0 else 1)
    for d in range(win - 1):
        acc_ref[d * tb:(d + 1) * tb, :] = state[d][2]
        c0_ref[d * tb:(d + 1) * tb, :] = jnp.full((tb, 1), -SKIP_LOG2, F32)
        c1_ref[d * tb:(d + 1) * tb, :] = jnp.full((tb, 1), -SKIP_LOG2, F32)
    o_ref[...] = acc_ref[0:s, :].astype(o_ref.dtype)

    def unfinished(c0, c1):
        return (jnp.max(jnp.maximum(c0, c1)) > -SKIP_LOG2).astype(jnp.int32)

    @pl.when(unfinished(worst, worst) > 0)
    def _():
        def finish(t, _):
            row0 = pl.multiple_of(t * tb, tb)
            q = q_ref[pl.ds(row0, tb), :]

            def one_block(carry):
                j, _, c0, c1, acc = carry
                k2, v2 = key_block(j)
                z = lax.dot_general(q, k2, NT_DIMS, preferred_element_type=F32)
                sp = softplus2(z)
                later = jnp.dot(sp.astype(BF16), tri, preferred_element_type=F32)
                w = jnp.exp2(z - sp + later + carry_lanes(c0, c1))
                acc = acc + jnp.dot(w.astype(BF16), v2, preferred_element_type=F32)
                s0, s1 = row_sums(sp)
                return j - 1, unfinished(c0 - s0, c1 - s1), c0 - s0, c1 - s1, acc

            c0, c1 = c0_ref[pl.ds(row0, tb), :], c1_ref[pl.ds(row0, tb), :]
            carry = (t - win, unfinished(c0, c1), c0, c1, acc_ref[pl.ds(row0, tb), :])
            carry = lax.while_loop(lambda c: (c[0] >= 0) & (c[1] > 0), one_block, carry)
            o_ref[pl.ds(row0, tb), :] = carry[4].astype(o_ref.dtype)
            return 0

        lax.fori_loop(win, nblk, finish, 0)


def _stick_breaking(q, kv, tb, win):
    b, s, _ = q.shape
    pairs = SB_WIDTH // LANES
    pad = (win - 1) * tb
    r = lax.broadcasted_iota(jnp.int32, (2 * tb, 2 * tb), 0)
    c = lax.broadcasted_iota(jnp.int32, (2 * tb, 2 * tb), 1)
    tri = -((r > c) & ((r < tb) == (c < tb))).astype(BF16)
    seq = lambda col: pl.BlockSpec((None, s, LANES), lambda i, p: (i, 0, col + p))
    return pl.pallas_call(
        functools.partial(_sb_kernel, tb=tb, win=win),
        out_shape=jax.ShapeDtypeStruct((b, s, SB_WIDTH), BF16),
        grid=(b, pairs),
        in_specs=[seq(0), seq(0), seq(pairs), _const_spec((2 * tb, 2 * tb))],
        out_specs=seq(0),
        scratch_shapes=[pltpu.VMEM((s + pad, LANES), BF16), pltpu.VMEM((s + pad, LANES), F32),
                        pltpu.VMEM((s + pad, 1), F32), pltpu.VMEM((s + pad, 1), F32)],
        compiler_params=pltpu.CompilerParams(
            dimension_semantics=("arbitrary", "arbitrary"), vmem_limit_bytes=VMEM_LIMIT),
        name="stick_breaking",
    )(q, kv, kv, tri)


def _sb_mixer_kernel(x_ref, sb_ref, qm_ref, k2_ref, v2_ref, wout_ref, o_ref, *, m):
    mem_out = _memory_attention(qm_ref[...], k2_ref, v2_ref, m)
    o_ref[...] = (x_ref[...]
                  + jnp.dot(sb_ref[...], wout_ref[0:SB_WIDTH, :], preferred_element_type=F32)
                  + jnp.dot(mem_out.astype(BF16), wout_ref[SB_WIDTH:, :], preferred_element_type=F32))


def _sb_mixer(x, sb_out, q, k2, v2, layer, w_out, tm):
    b, s, d = x.shape
    m = k2.shape[3] // 2
    pairs = MEM_WIDTH // LANES
    kv_spec = pl.BlockSpec((None, None, pairs, 2 * m, LANES), lambda i, t: (layer, i, 0, 0, 0))
    return pl.pallas_call(
        functools.partial(_sb_mixer_kernel, m=m),
        out_shape=jax.ShapeDtypeStruct((b, s, d), F32),
        grid=(b, s // tm),
        in_specs=[
            pl.BlockSpec((None, tm, d), lambda i, t: (i, t, 0)),
            pl.BlockSpec((None, tm, SB_WIDTH), lambda i, t: (i, t, 0)),
            pl.BlockSpec((None, tm, MEM_WIDTH), lambda i, t: (i, t, SB_WIDTH // MEM_WIDTH)),
            kv_spec, kv_spec,
            _const_spec((MIX_WIDTH, d)),
        ],
        out_specs=pl.BlockSpec((None, tm, d), lambda i, t: (i, t, 0)),
        compiler_params=pltpu.CompilerParams(
            dimension_semantics=("arbitrary", "arbitrary"), vmem_limit_bytes=VMEM_LIMIT),
        name="sb_mixer",
    )(x, sb_out, q, k2, v2, w_out)


def _block_diag(w_group):
    g, c, _ = w_group.shape
    eye = jnp.eye(g, dtype=w_group.dtype)
    return jnp.einsum("gcd,gh->gchd", w_group, eye).reshape(g * c, g * c)


def kernel(x, mem, mem_norm, a_norm_mix, a_w_in, a_w_group, a_scale, a_w_mem_kv, a_w_out, a_norm_ffn,
           a_w_gu, a_w_down, kv_norm, w_kv, b_norm_mix, b_w_q, b_w_mem_kv, b_w_out, b_norm_ffn, b_w_gu,
           b_w_down, final_norm):
    b, s, d = x.shape
    na, nb = a_w_in.shape[0], b_w_q.shape[0]
    tm = min(512, s)
    tb = min(SB_TILE, s)
    win = min(SB_WINDOW, s // tb)

    k2, v2 = _memory_kv(mem, mem_norm, jnp.concatenate([a_w_mem_kv, b_w_mem_kv], axis=0).astype(BF16))

    for i in range(na):
        x = _pool_mixer(x, a_norm_mix[i], a_w_in[i].astype(BF16), _block_diag(a_w_group[i]).astype(BF16),
                        a_scale[i], k2, v2, i, a_w_out[i].astype(BF16), tm)
        last = nb == 0 and i == na - 1
        x = _ffn(x.reshape(b * s, d), a_norm_ffn[i], a_w_gu[i].astype(BF16), a_w_down[i].astype(BF16),
                 final_norm, tm, last).reshape(b, s, d)

    kv = None
    for j in range(nb):
        kv_j, q = _qkv_proj(x.reshape(b * s, d), kv_norm, w_kv.astype(BF16) if j == 0 else None,
                            b_norm_mix[j], b_w_q[j].astype(BF16), tm)
        if j == 0:
            kv = kv_j.reshape(b, s, 2 * SB_WIDTH)
        q = q.reshape(b, s, MIX_WIDTH)
        sb_out = _stick_breaking(q, kv, tb, win)
        x = _sb_mixer(x, sb_out, q, k2, v2, na + j, b_w_out[j].astype(BF16), tm)
        x = _ffn(x.reshape(b * s, d), b_norm_ffn[j], b_w_gu[j].astype(BF16), b_w_down[j].astype(BF16),
                 final_norm, tm, j == nb - 1).reshape(b, s, d)
    return x
```

```python
import functools

import jax
import jax.numpy as jnp
from jax import lax
from jax.experimental import pallas as pl
from jax.experimental.pallas import tpu as pltpu

HEAD_DIM = 64
N_SB_HEADS = 12
SB_WIDTH = N_SB_HEADS * HEAD_DIM
N_MEM_HEADS = 4
MEM_WIDTH = N_MEM_HEADS * HEAD_DIM
MIX_WIDTH = SB_WIDTH + MEM_WIDTH
POOL_WINDOWS = (2, 4, 8, 16)
POOL_WIDTH = SB_WIDTH
POOL_GROUP = POOL_WIDTH // len(POOL_WINDOWS)
EPS = 1e-6
LOG2_E = 1.4426950408889634
QK_SCALE_LOG2 = HEAD_DIM ** -0.5 * LOG2_E

LANES = 128
POOL_PAD = 32
FF_CHUNK = 256
SB_TILE = 128
SB_UNROLL = 16
SKIP_LOG2 = 151.0
VMEM_LIMIT = 56 * 1024 * 1024

F32 = jnp.float32
BF16 = jnp.bfloat16
NT_DIMS = (((1,), (1,)), ((), ()))


def _const_spec(shape):
    nd = len(shape)
    return pl.BlockSpec(shape, lambda *_: (0,) * nd, pipeline_mode=pl.Buffered(1))


def _rms_scale(x):
    return x * lax.rsqrt(jnp.mean(x * x, axis=-1, keepdims=True) + EPS)


def _split_heads(t, rows):
    lane = lax.broadcasted_iota(jnp.int32, (rows, LANES), 1)
    first = lane < HEAD_DIM
    zero = jnp.zeros_like(t)
    return jnp.concatenate([jnp.where(first, t, zero), jnp.where(first, zero, t)], axis=0)


def _memkv_kernel(mem_ref, g_ref, w_ref, k2_ref, v2_ref, *, m):
    h = (_rms_scale(mem_ref[...]) * g_ref[...]).astype(BF16)
    kv = jnp.dot(h, w_ref[...], preferred_element_type=F32)
    for p in range(MEM_WIDTH // LANES):
        k2_ref[p] = _split_heads(kv[:, p * LANES:(p + 1) * LANES], m).astype(BF16)
        v2_ref[p] = _split_heads(kv[:, MEM_WIDTH + p * LANES:MEM_WIDTH + (p + 1) * LANES], m).astype(BF16)


def _memory_kv(mem, mem_norm, w_mem_kv):
    b, m, d = mem.shape
    n_layers = w_mem_kv.shape[0]
    pairs = MEM_WIDTH // LANES
    out = jax.ShapeDtypeStruct((n_layers, b, pairs, 2 * m, LANES), BF16)
    out_spec = pl.BlockSpec((None, None, pairs, 2 * m, LANES), lambda l, i: (l, i, 0, 0, 0))
    return pl.pallas_call(
        functools.partial(_memkv_kernel, m=m),
        out_shape=(out, out),
        grid=(n_layers, b),
        in_specs=[
            pl.BlockSpec((None, m, d), lambda l, i: (i, 0, 0)),
            pl.BlockSpec((1, d), lambda l, i: (0, 0)),
            pl.BlockSpec((None, d, 2 * MEM_WIDTH), lambda l, i: (l, 0, 0)),
        ],
        out_specs=(out_spec, out_spec),
        name="memory_kv",
    )(mem, mem_norm.reshape(1, d), w_mem_kv)


def _memory_attention(qm, k2_ref, v2_ref, m):
    t = qm.shape[0]
    lane = lax.broadcasted_iota(jnp.int32, (t, LANES), 1)
    outs = []
    for p in range(MEM_WIDTH // LANES):
        logits = lax.dot_general(qm[:, p * LANES:(p + 1) * LANES], k2_ref[p], NT_DIMS,
                                 preferred_element_type=F32)
        probs, inv = [], []
        for hh in range(2):
            lg = logits[:, hh * m:(hh + 1) * m]
            pe = jnp.exp2(lg - jnp.max(lg, axis=1, keepdims=True))
            probs.append(pe)
            inv.append(1.0 / jnp.sum(pe, axis=1, keepdims=True))
        pv = jnp.dot(jnp.concatenate(probs, axis=1).astype(BF16), v2_ref[p],
                     preferred_element_type=F32)
        outs.append(pv * jnp.where(lane < HEAD_DIM, inv[0], inv[1]))
    return jnp.concatenate(outs, axis=1)


def _pool_mixer_kernel(x_ref, g_ref, win_ref, wgrp_ref, scale_ref, k2_ref, v2_ref, wout_ref, o_ref,
                       e_ref, s2_ref, s4_ref, s8_ref, *, tm, m):
    t = pl.program_id(1)
    x = x_ref[...]
    h = (_rms_scale(x) * g_ref[...]).astype(BF16)
    proj = jnp.dot(h, win_ref[...], preferred_element_type=F32)
    u = proj[:, :POOL_WIDTH]

    @pl.when(t == 0)
    def _():
        e_ref[0:POOL_PAD, :] = jnp.zeros((POOL_PAD, POOL_WIDTH), F32)

    @pl.when(t > 0)
    def _():
        e_ref[POOL_PAD - 16:POOL_PAD, :] = e_ref[tm + POOL_PAD - 16:tm + POOL_PAD, :]

    e_ref[POOL_PAD:, :] = u
    n = tm + POOL_PAD
    s2_ref[8:, :] = e_ref[8:, :] + e_ref[7:n - 1, :]
    s4_ref[16:, 128:] = s2_ref[16:, 128:] + s2_ref[14:n - 2, 128:]
    s8_ref[24:, 384:] = s4_ref[24:, 384:] + s4_ref[20:n - 4, 384:]
    s16 = s8_ref[32:, 512:] + s8_ref[24:n - 8, 512:]

    pos1 = t * tm + lax.broadcasted_iota(jnp.int32, (tm, 1), 0) + 1
    inv = [1.0 / jnp.minimum(pos1, w).astype(F32) for w in POOL_WINDOWS]
    lane = lax.broadcasted_iota(jnp.int32, (tm, LANES), 1)
    low = lane < (POOL_GROUP - LANES)
    pooled = jnp.concatenate([
        s2_ref[POOL_PAD:, 0:128] * inv[0],
        jnp.where(low, s2_ref[POOL_PAD:, 128:256] * inv[0], s4_ref[POOL_PAD:, 128:256] * inv[1]),
        s4_ref[POOL_PAD:, 256:384] * inv[1],
        s8_ref[POOL_PAD:, 384:512] * inv[2],
        jnp.where(low, s8_ref[POOL_PAD:, 512:640] * inv[2], s16[:, 0:128] * inv[3]),
        s16[:, 128:256] * inv[3],
    ], axis=1) - u

    grouped = jnp.dot(pooled.astype(BF16), wgrp_ref[...], preferred_element_type=F32) * scale_ref[...]
    qm = (proj[:, POOL_WIDTH:] * QK_SCALE_LOG2).astype(BF16)
    mem_out = _memory_attention(qm, k2_ref, v2_ref, m)
    o_ref[...] = (x
                  + jnp.dot(grouped.astype(BF16), wout_ref[0:POOL_WIDTH, :], preferred_element_type=F32)
                  + jnp.dot(mem_out.astype(BF16), wout_ref[POOL_WIDTH:, :], preferred_element_type=F32))


def _pool_mixer(x, g, w_in, w_group_full, scale, k2, v2, layer, w_out, tm):
    b, s, d = x.shape
    m = k2.shape[3] // 2
    pairs = MEM_WIDTH // LANES
    kv_spec = pl.BlockSpec((None, None, pairs, 2 * m, LANES), lambda i, t: (layer, i, 0, 0, 0))
    return pl.pallas_call(
        functools.partial(_pool_mixer_kernel, tm=tm, m=m),
        out_shape=jax.ShapeDtypeStruct((b, s, d), F32),
        grid=(b, s // tm),
        in_specs=[
            pl.BlockSpec((None, tm, d), lambda i, t: (i, t, 0)),
            _const_spec((1, d)),
            _const_spec((d, MIX_WIDTH)),
            _const_spec((POOL_WIDTH, POOL_WIDTH)),
            _const_spec((1, POOL_WIDTH)),
            kv_spec, kv_spec,
            _const_spec((MIX_WIDTH, d)),
        ],
        out_specs=pl.BlockSpec((None, tm, d), lambda i, t: (i, t, 0)),
        scratch_shapes=[pltpu.VMEM((tm + POOL_PAD, POOL_WIDTH), F32)] * 4,
        compiler_params=pltpu.CompilerParams(
            dimension_semantics=("arbitrary", "arbitrary"), vmem_limit_bytes=VMEM_LIMIT),
        name="pool_mixer",
    )(x, g.reshape(1, d), w_in, w_group_full, scale.reshape(1, POOL_WIDTH), k2, v2, w_out)


def _ffn_kernel(x_ref, g_ref, wgu_ref, wd_ref, gf_ref, o_ref, h_ref, *, final_norm):
    x = x_ref[...]
    h_ref[...] = (_rms_scale(x) * g_ref[...]).astype(BF16)
    d_ff = wd_ref.shape[0]
    for c in range(d_ff // FF_CHUNK):
        lo, hi = c * FF_CHUNK, (c + 1) * FF_CHUNK
        h = h_ref[...]
        gate = jnp.dot(h, wgu_ref[:, lo:hi], preferred_element_type=F32)
        up = jnp.dot(h, wgu_ref[:, d_ff + lo:d_ff + hi], preferred_element_type=F32)
        act = gate * (1.0 / (1.0 + jnp.exp(-gate))) * up
        down = jnp.dot(act.astype(BF16), wd_ref[lo:hi, :], preferred_element_type=F32)
        o_ref[...] = (x_ref[...] if c == 0 else o_ref[...]) + down
    if final_norm:
        o_ref[...] = _rms_scale(o_ref[...]) * gf_ref[...]


def _ffn(x2d, g, w_gu, w_down, g_final, tm, final_norm):
    n, d = x2d.shape
    d_ff = w_down.shape[0]
    return pl.pallas_call(
        functools.partial(_ffn_kernel, final_norm=final_norm),
        out_shape=jax.ShapeDtypeStruct((n, d), F32),
        grid=(n // tm,),
        in_specs=[
            pl.BlockSpec((tm, d), lambda i: (i, 0)),
            _const_spec((1, d)),
            _const_spec((d, 2 * d_ff)),
            _const_spec((d_ff, d)),
            _const_spec((1, d)),
        ],
        out_specs=pl.BlockSpec((tm, d), lambda i: (i, 0)),
        scratch_shapes=[pltpu.VMEM((tm, d), BF16)],
        compiler_params=pltpu.CompilerParams(
            dimension_semantics=("arbitrary",), vmem_limit_bytes=VMEM_LIMIT),
        name="ffn_final" if final_norm else "ffn",
    )(x2d, g.reshape(1, d), w_gu, w_down, g_final.reshape(1, d))


def _qkv_kernel(x_ref, gkv_ref, gq_ref, wkv_ref, wq_ref, kv_ref, q_ref):
    xn = _rms_scale(x_ref[...])
    kv = jnp.dot((xn * gkv_ref[...]).astype(BF16), wkv_ref[...], preferred_element_type=F32)
    kv_ref[...] = kv.astype(BF16)
    q = jnp.dot((xn * gq_ref[...]).astype(BF16), wq_ref[...], preferred_element_type=F32)
    q_ref[...] = (q * QK_SCALE_LOG2).astype(BF16)


def _q_kernel(x_ref, gq_ref, wq_ref, q_ref):
    h = (_rms_scale(x_ref[...]) * gq_ref[...]).astype(BF16)
    q_ref[...] = (jnp.dot(h, wq_ref[...], preferred_element_type=F32) * QK_SCALE_LOG2).astype(BF16)


def _qkv_proj(x2d, g_kv, w_kv, g_q, w_q, tm):
    n, d = x2d.shape
    row = lambda w: pl.BlockSpec((tm, w), lambda i: (i, 0))
    params = pltpu.CompilerParams(dimension_semantics=("arbitrary",), vmem_limit_bytes=VMEM_LIMIT)
    if w_kv is None:
        return None, pl.pallas_call(
            _q_kernel, out_shape=jax.ShapeDtypeStruct((n, MIX_WIDTH), BF16), grid=(n // tm,),
            in_specs=[row(d), _const_spec((1, d)), _const_spec((d, MIX_WIDTH))],
            out_specs=row(MIX_WIDTH), compiler_params=params, name="q_proj",
        )(x2d, g_q.reshape(1, d), w_q)
    return pl.pallas_call(
        _qkv_kernel,
        out_shape=(jax.ShapeDtypeStruct((n, 2 * SB_WIDTH), BF16), jax.ShapeDtypeStruct((n, MIX_WIDTH), BF16)),
        grid=(n // tm,),
        in_specs=[row(d), _const_spec((1, d)), _const_spec((1, d)),
                  _const_spec((d, 2 * SB_WIDTH)), _const_spec((d, MIX_WIDTH))],
        out_specs=(row(2 * SB_WIDTH), row(MIX_WIDTH)),
        compiler_params=params, name="qkv_proj",
    )(x2d, g_kv.reshape(1, d), g_q.reshape(1, d), w_kv, w_q)


def _sb_kernel(q_ref, k_ref, v_ref, tri_ref, o_ref, qpad_ref, acc_ref, c0_ref, c1_ref, *, tb):
    s = q_ref.shape[0]
    nblk = s // tb
    hb = tb // 2
    pad = tb + hb
    tri = tri_ref[...]
    rows = lax.broadcasted_iota(jnp.int32, (tb, 2 * tb), 0)
    cols = lax.broadcasted_iota(jnp.int32, (tb, 2 * tb), 1)
    before = jnp.where(cols < tb, cols, cols - tb) < rows

    def softplus2(z):
        pos = jnp.maximum(z, 0.0)
        neg = z - pos
        l = jnp.log(1.0 + jnp.exp2(neg - pos)) * LOG2_E
        return pos + l, neg - l

    def carry_lanes(c0, c1):
        n = c0.shape[0]
        return jnp.concatenate([jnp.broadcast_to(c0, (n, tb)), jnp.broadcast_to(c1, (n, tb))], axis=1)

    def row_sums(sp):
        return jnp.sum(sp[:, :tb], axis=1, keepdims=True), jnp.sum(sp[:, tb:], axis=1, keepdims=True)

    def key_block(j):
        start = pl.multiple_of(j * tb, tb)
        return (_split_heads(k_ref[pl.ds(start, tb), :], tb),
                _split_heads(v_ref[pl.ds(start, tb), :], tb))

    def unfinished(c0, c1):
        return (jnp.max(jnp.maximum(c0, c1)) > -SKIP_LOG2).astype(jnp.int32)

    qpad_ref[0:s, :] = q_ref[...]
    qpad_ref[s:, :] = jnp.zeros((pad, LANES), q_ref.dtype)

    def sweep(i, carry):
        (a_c0, a_c1, a_acc), (b_c0, b_c1, b_acc), worst = carry
        j = nblk - 1 - i
        row0 = pl.multiple_of(j * tb, tb)
        k2, v2 = key_block(j)
        q = qpad_ref[pl.ds(row0, 2 * tb + hb), :]
        z = lax.dot_general(q, k2, NT_DIMS, preferred_element_type=F32)
        sp0, lb0 = softplus2(z[0:tb])
        sp0 = jnp.where(before, sp0, 0.0)
        sp1, lb1 = softplus2(z[tb:2 * tb])
        sp2, lb2 = softplus2(z[2 * tb:])
        later = jnp.dot(jnp.concatenate([sp0, sp1, sp2], axis=0).astype(BF16), tri,
                        preferred_element_type=F32)
        w0 = jnp.where(before, jnp.exp2(lb0 + later[0:tb]), 0.0)
        w1 = jnp.exp2(lb1 + later[tb:2 * tb] + carry_lanes(a_c0, a_c1))
        w2 = jnp.exp2(lb2 + later[2 * tb:] + carry_lanes(b_c0, b_c1))
        pv = jnp.dot(jnp.concatenate([w0, w1, w2], axis=0).astype(BF16), v2, preferred_element_type=F32)
        s00, s01 = row_sums(sp0)
        s10, s11 = row_sums(sp1)
        s20, s21 = row_sums(sp2)
        a_c0, a_c1, a_acc = a_c0 - s10, a_c1 - s11, a_acc + pv[tb:2 * tb]
        b_c0, b_c1, b_acc = b_c0 - s20, b_c1 - s21, b_acc + pv[2 * tb:]
        out0 = pl.multiple_of(row0 + pad, hb)
        c0 = jnp.concatenate([a_c0[hb:], b_c0], axis=0)
        c1 = jnp.concatenate([a_c1[hb:], b_c1], axis=0)
        acc_ref[pl.ds(out0, tb), :] = jnp.concatenate([a_acc[hb:], b_acc], axis=0)
        c0_ref[pl.ds(out0, tb), :] = c0
        c1_ref[pl.ds(out0, tb), :] = c1
        live = (out0 + lax.broadcasted_iota(jnp.int32, (tb, 1), 0) < s) & (j >= 1)
        worst = jnp.where(live, jnp.maximum(worst, jnp.maximum(c0, c1)), worst)
        return (-s00, -s01, pv[0:tb]), (a_c0[:hb], a_c1[:hb], a_acc[:hb]), worst

    def zeros(n):
        return jnp.zeros((n, 1), F32), jnp.zeros((n, 1), F32), jnp.zeros((n, LANES), F32)

    init = (zeros(tb), zeros(hb), jnp.full((tb, 1), -SKIP_LOG2, F32))
    a, b, worst = lax.fori_loop(0, nblk, sweep, init, unroll=SB_UNROLL if nblk % SB_UNROLL == 0 else 1)
    acc_ref[0:tb, :] = a[2]
    acc_ref[tb:pad, :] = b[2]
    o_ref[...] = acc_ref[0:s, :].astype(o_ref.dtype)

    @pl.when(unfinished(worst, worst) > 0)
    def _():
        def finish(hh, _):
            row0 = pl.multiple_of(hh * hb, hb)
            q = q_ref[pl.ds(row0, hb), :]

            def one_block(carry):
                j, _, c0, c1, acc = carry
                k2, v2 = key_block(j)
                z = lax.dot_general(q, k2, NT_DIMS, preferred_element_type=F32)
                sp, log_beta = softplus2(z)
                later = jnp.dot(sp.astype(BF16), tri, preferred_element_type=F32)
                w = jnp.exp2(log_beta + later + carry_lanes(c0, c1))
                acc = acc + jnp.dot(w.astype(BF16), v2, preferred_element_type=F32)
                s0, s1 = row_sums(sp)
                return j - 1, unfinished(c0 - s0, c1 - s1), c0 - s0, c1 - s1, acc

            first_unmet = hh // 2 - 3 + hh % 2
            c0, c1 = c0_ref[pl.ds(row0, hb), :], c1_ref[pl.ds(row0, hb), :]
            carry = (first_unmet, unfinished(c0, c1), c0, c1, acc_ref[pl.ds(row0, hb), :])
            carry = lax.while_loop(lambda c: (c[0] >= 0) & (c[1] > 0), one_block, carry)
            o_ref[pl.ds(row0, hb), :] = carry[4].astype(o_ref.dtype)
            return 0

        lax.fori_loop(3, 2 * nblk, finish, 0)


def _stick_breaking(q, kv, tb):
    b, s, _ = q.shape
    pairs = SB_WIDTH // LANES
    pad = tb + tb // 2
    r = lax.broadcasted_iota(jnp.int32, (2 * tb, 2 * tb), 0)
    c = lax.broadcasted_iota(jnp.int32, (2 * tb, 2 * tb), 1)
    tri = -((r > c) & ((r < tb) == (c < tb))).astype(BF16)
    seq = lambda col: pl.BlockSpec((None, s, LANES), lambda i, p: (i, 0, col + p))
    return pl.pallas_call(
        functools.partial(_sb_kernel, tb=tb),
        out_shape=jax.ShapeDtypeStruct((b, s, SB_WIDTH), BF16),
        grid=(b, pairs),
        in_specs=[seq(0), seq(0), seq(pairs), _const_spec((2 * tb, 2 * tb))],
        out_specs=seq(0),
        scratch_shapes=[pltpu.VMEM((s + pad, LANES), BF16), pltpu.VMEM((s + pad, LANES), F32),
                        pltpu.VMEM((s + pad, 1), F32), pltpu.VMEM((s + pad, 1), F32)],
        compiler_params=pltpu.CompilerParams(
            dimension_semantics=("arbitrary", "arbitrary"), vmem_limit_bytes=VMEM_LIMIT),
        name="stick_breaking",
    )(q, kv, kv, tri)


def _sb_mixer_kernel(x_ref, sb_ref, qm_ref, k2_ref, v2_ref, wout_ref, o_ref, *, m):
    mem_out = _memory_attention(qm_ref[...], k2_ref, v2_ref, m)
    o_ref[...] = (x_ref[...]
                  + jnp.dot(sb_ref[...], wout_ref[0:SB_WIDTH, :], preferred_element_type=F32)
                  + jnp.dot(mem_out.astype(BF16), wout_ref[SB_WIDTH:, :], preferred_element_type=F32))


def _sb_mixer(x, sb_out, q, k2, v2, layer, w_out, tm):
    b, s, d = x.shape
    m = k2.shape[3] // 2
    pairs = MEM_WIDTH // LANES
    kv_spec = pl.BlockSpec((None, None, pairs, 2 * m, LANES), lambda i, t: (layer, i, 0, 0, 0))
    return pl.pallas_call(
        functools.partial(_sb_mixer_kernel, m=m),
        out_shape=jax.ShapeDtypeStruct((b, s, d), F32),
        grid=(b, s // tm),
        in_specs=[
            pl.BlockSpec((None, tm, d), lambda i, t: (i, t, 0)),
            pl.BlockSpec((None, tm, SB_WIDTH), lambda i, t: (i, t, 0)),
            pl.BlockSpec((None, tm, MEM_WIDTH), lambda i, t: (i, t, SB_WIDTH // MEM_WIDTH)),
            kv_spec, kv_spec,
            _const_spec((MIX_WIDTH, d)),
        ],
        out_specs=pl.BlockSpec((None, tm, d), lambda i, t: (i, t, 0)),
        compiler_params=pltpu.CompilerParams(
            dimension_semantics=("arbitrary", "arbitrary"), vmem_limit_bytes=VMEM_LIMIT),
        name="sb_mixer",
    )(x, sb_out, q, k2, v2, w_out)


def _block_diag(w_group):
    g, c, _ = w_group.shape
    eye = jnp.eye(g, dtype=w_group.dtype)
    return jnp.einsum("gcd,gh->gchd", w_group, eye).reshape(g * c, g * c)


def kernel(x, mem, mem_norm, a_norm_mix, a_w_in, a_w_group, a_scale, a_w_mem_kv, a_w_out, a_norm_ffn,
           a_w_gu, a_w_down, kv_norm, w_kv, b_norm_mix, b_w_q, b_w_mem_kv, b_w_out, b_norm_ffn, b_w_gu,
           b_w_down, final_norm):
    b, s, d = x.shape
    na, nb = a_w_in.shape[0], b_w_q.shape[0]
    tm = min(512, s)
    tb = SB_TILE

    k2, v2 = _memory_kv(mem, mem_norm, jnp.concatenate([a_w_mem_kv, b_w_mem_kv], axis=0).astype(BF16))

    for i in range(na):
        x = _pool_mixer(x, a_norm_mix[i], a_w_in[i].astype(BF16), _block_diag(a_w_group[i]).astype(BF16),
                        a_scale[i], k2, v2, i, a_w_out[i].astype(BF16), tm)
        last = nb == 0 and i == na - 1
        x = _ffn(x.reshape(b * s, d), a_norm_ffn[i], a_w_gu[i].astype(BF16), a_w_down[i].astype(BF16),
                 final_norm, tm, last).reshape(b, s, d)

    kv = None
    for j in range(nb):
        kv_j, q = _qkv_proj(x.reshape(b * s, d), kv_norm, w_kv.astype(BF16) if j == 0 else None,
                            b_norm_mix[j], b_w_q[j].astype(BF16), tm)
        if j == 0:
            kv = kv_j.reshape(b, s, 2 * SB_WIDTH)
        q = q.reshape(b, s, MIX_WIDTH)
        sb_out = _stick_breaking(q, kv, tb)
        x = _sb_mixer(x, sb_out, q, k2, v2, na + j, b_w_out[j].astype(BF16), tm)
        x = _ffn(x.reshape(b * s, d), b_norm_ffn[j], b_w_gu[j].astype(BF16), b_w_down[j].astype(BF16),
                 final_norm, tm, j == nb - 1).reshape(b, s, d)
    return x
```

```python
import functools

import jax
import jax.numpy as jnp
from jax import lax
from jax.experimental import pallas as pl
from jax.experimental.pallas import tpu as pltpu

HEAD_DIM = 64
N_SB_HEADS = 12
SB_WIDTH = N_SB_HEADS * HEAD_DIM
N_MEM_HEADS = 4
MEM_WIDTH = N_MEM_HEADS * HEAD_DIM
MIX_WIDTH = SB_WIDTH + MEM_WIDTH
POOL_WINDOWS = (2, 4, 8, 16)
POOL_WIDTH = SB_WIDTH
POOL_GROUP = POOL_WIDTH // len(POOL_WINDOWS)
EPS = 1e-6
LOG2_E = 1.4426950408889634
QK_SCALE_LOG2 = HEAD_DIM ** -0.5 * LOG2_E

LANES = 128
ROW_TILE = 512
POOL_PAD = 32
FF_CHUNK = 256
SB_TILE = 128
SB_UNROLL = 16
SKIP_LOG2 = 151.0
MASKED = 1e4
VMEM_LIMIT = 56 * 1024 * 1024

F32 = jnp.float32
BF16 = jnp.bfloat16
NT_DIMS = (((1,), (1,)), ((), ()))


def _const_spec(shape):
    nd = len(shape)
    return pl.BlockSpec(shape, lambda *_: (0,) * nd, pipeline_mode=pl.Buffered(1))


def _rms_scale(x):
    return x * lax.rsqrt(jnp.mean(x * x, axis=-1, keepdims=True) + EPS)


def _split_heads(t, rows):
    lane = lax.broadcasted_iota(jnp.int32, (rows, LANES), 1)
    first = lane < HEAD_DIM
    zero = jnp.zeros_like(t)
    return jnp.concatenate([jnp.where(first, t, zero), jnp.where(first, zero, t)], axis=0)


def _memkv_kernel(mem_ref, g_ref, w_ref, k2_ref, v2_ref, *, m):
    h = (_rms_scale(mem_ref[...]) * g_ref[...]).astype(BF16)
    kv = jnp.dot(h, w_ref[...], preferred_element_type=F32)
    for p in range(MEM_WIDTH // LANES):
        k2_ref[p] = _split_heads(kv[:, p * LANES:(p + 1) * LANES], m).astype(BF16)
        v2_ref[p] = _split_heads(kv[:, MEM_WIDTH + p * LANES:MEM_WIDTH + (p + 1) * LANES], m).astype(BF16)


def _memory_kv(mem, mem_norm, w_mem_kv):
    b, m, d = mem.shape
    n_layers = w_mem_kv.shape[0]
    pairs = MEM_WIDTH // LANES
    out = jax.ShapeDtypeStruct((n_layers, b, pairs, 2 * m, LANES), BF16)
    out_spec = pl.BlockSpec((None, None, pairs, 2 * m, LANES), lambda l, i: (l, i, 0, 0, 0))
    return pl.pallas_call(
        functools.partial(_memkv_kernel, m=m),
        out_shape=(out, out),
        grid=(n_layers, b),
        in_specs=[
            pl.BlockSpec((None, m, d), lambda l, i: (i, 0, 0)),
            pl.BlockSpec((1, d), lambda l, i: (0, 0)),
            pl.BlockSpec((None, d, 2 * MEM_WIDTH), lambda l, i: (l, 0, 0)),
        ],
        out_specs=(out_spec, out_spec),
        name="memory_kv",
    )(mem, mem_norm.reshape(1, d), w_mem_kv)


def _memory_kv_spec(k2, layer):
    return pl.BlockSpec((None, None) + k2.shape[2:], lambda i, t: (layer, i, 0, 0, 0))


def _memory_attention(qm, k2_ref, v2_ref):
    t = qm.shape[0]
    m = k2_ref.shape[1] // 2
    lane = lax.broadcasted_iota(jnp.int32, (t, LANES), 1)
    outs = []
    for p in range(MEM_WIDTH // LANES):
        logits = lax.dot_general(qm[:, p * LANES:(p + 1) * LANES], k2_ref[p], NT_DIMS,
                                 preferred_element_type=F32)
        probs, inv = [], []
        for hh in range(2):
            lg = logits[:, hh * m:(hh + 1) * m]
            pe = jnp.exp2(lg - jnp.max(lg, axis=1, keepdims=True))
            probs.append(pe)
            inv.append(1.0 / jnp.sum(pe, axis=1, keepdims=True))
        pv = jnp.dot(jnp.concatenate(probs, axis=1).astype(BF16), v2_ref[p],
                     preferred_element_type=F32)
        outs.append(pv * jnp.where(lane < HEAD_DIM, inv[0], inv[1]))
    return jnp.concatenate(outs, axis=1)


def _ffn_in_place(o_ref, h_ref, g_ref, wgu_ref, wd_ref):
    h_ref[...] = (_rms_scale(o_ref[...]) * g_ref[...]).astype(BF16)
    d_ff = wd_ref.shape[0]
    for c in range(d_ff // FF_CHUNK):
        lo, hi = c * FF_CHUNK, (c + 1) * FF_CHUNK
        h = h_ref[...]
        gate = jnp.dot(h, wgu_ref[:, lo:hi], preferred_element_type=F32)
        up = jnp.dot(h, wgu_ref[:, d_ff + lo:d_ff + hi], preferred_element_type=F32)
        act = gate * (1.0 / (1.0 + jnp.exp(-gate))) * up
        o_ref[...] += jnp.dot(act.astype(BF16), wd_ref[lo:hi, :], preferred_element_type=F32)


def _fold_kernel(win_ref, wgrp_ref, scale_ref, o_ref):
    a = win_ref[:, :POOL_WIDTH]
    b = wgrp_ref[...]
    a_hi, b_hi = a.astype(BF16), b.astype(BF16)
    a_lo, b_lo = (a - a_hi.astype(F32)).astype(BF16), (b - b_hi.astype(F32)).astype(BF16)
    prod = (jnp.dot(a_hi, b_hi, preferred_element_type=F32) + jnp.dot(a_hi, b_lo, preferred_element_type=F32)
            + jnp.dot(a_lo, b_hi, preferred_element_type=F32))
    o_ref[:, :POOL_WIDTH] = (prod * scale_ref[...]).astype(BF16)
    o_ref[:, POOL_WIDTH:] = win_ref[:, POOL_WIDTH:].astype(BF16)


def _fold_pool_weights(w_in, w_group_full, scale):
    d = w_in.shape[0]
    tr = min(256, d)
    return pl.pallas_call(
        _fold_kernel,
        out_shape=jax.ShapeDtypeStruct((d, MIX_WIDTH), BF16),
        grid=(d // tr,),
        in_specs=[pl.BlockSpec((tr, MIX_WIDTH), lambda i: (i, 0)),
                  _const_spec((POOL_WIDTH, POOL_WIDTH)), _const_spec((1, POOL_WIDTH))],
        out_specs=pl.BlockSpec((tr, MIX_WIDTH), lambda i: (i, 0)),
        name="fold_pool_weights",
    )(w_in, w_group_full, scale.reshape(1, POOL_WIDTH))


def _qkv_from(x, gkv_ref, gq_ref, wkv_ref, wq_ref, kv_ref, q_ref):
    xn = _rms_scale(x)
    kv = jnp.dot((xn * gkv_ref[...]).astype(BF16), wkv_ref[...], preferred_element_type=F32)
    kv_ref[...] = kv.astype(BF16)
    q = jnp.dot((xn * gq_ref[...]).astype(BF16), wq_ref[...], preferred_element_type=F32)
    q_ref[...] = (q * QK_SCALE_LOG2).astype(BF16)


def _pool_layer_kernel(x_ref, g_ref, win_ref, k2_ref, v2_ref, wout_ref, gf_ref, wgu_ref, wd_ref, *rest,
                       tm, with_qkv):
    if with_qkv:
        gkv_ref, gq_ref, wkv_ref, wq_ref, o_ref, kv_ref, q_ref, e_ref, s2_ref, s4_ref, s8_ref, h_ref = rest
    else:
        o_ref, e_ref, s2_ref, s4_ref, s8_ref, h_ref = rest
    t = pl.program_id(1)
    x = x_ref[...]
    h = (_rms_scale(x) * g_ref[...]).astype(BF16)
    proj = jnp.dot(h, win_ref[...], preferred_element_type=F32)
    u = proj[:, :POOL_WIDTH]

    @pl.when(t == 0)
    def _():
        e_ref[0:POOL_PAD, :] = jnp.zeros((POOL_PAD, POOL_WIDTH), F32)

    @pl.when(t > 0)
    def _():
        e_ref[POOL_PAD - 16:POOL_PAD, :] = e_ref[tm + POOL_PAD - 16:tm + POOL_PAD, :]

    e_ref[POOL_PAD:, :] = u
    n = tm + POOL_PAD
    s2_ref[8:, :] = e_ref[8:, :] + e_ref[7:n - 1, :]
    s4_ref[16:, 128:] = s2_ref[16:, 128:] + s2_ref[14:n - 2, 128:]
    s8_ref[24:, 384:] = s4_ref[24:, 384:] + s4_ref[20:n - 4, 384:]
    s16 = s8_ref[32:, 512:] + s8_ref[24:n - 8, 512:]

    pos1 = t * tm + lax.broadcasted_iota(jnp.int32, (tm, 1), 0) + 1
    inv = [1.0 / jnp.minimum(pos1, w).astype(F32) for w in POOL_WINDOWS]
    lane = lax.broadcasted_iota(jnp.int32, (tm, LANES), 1)
    low = lane < (POOL_GROUP - LANES)
    grouped = jnp.concatenate([
        s2_ref[POOL_PAD:, 0:128] * inv[0],
        jnp.where(low, s2_ref[POOL_PAD:, 128:256] * inv[0], s4_ref[POOL_PAD:, 128:256] * inv[1]),
        s4_ref[POOL_PAD:, 256:384] * inv[1],
        s8_ref[POOL_PAD:, 384:512] * inv[2],
        jnp.where(low, s8_ref[POOL_PAD:, 512:640] * inv[2], s16[:, 0:128] * inv[3]),
        s16[:, 128:256] * inv[3],
    ], axis=1) - u

    qm = (proj[:, POOL_WIDTH:] * QK_SCALE_LOG2).astype(BF16)
    mem_out = _memory_attention(qm, k2_ref, v2_ref)
    o_ref[...] = (x
                  + jnp.dot(grouped.astype(BF16), wout_ref[0:POOL_WIDTH, :], preferred_element_type=F32)
                  + jnp.dot(mem_out.astype(BF16), wout_ref[POOL_WIDTH:, :], preferred_element_type=F32))

    @pl.when(t >= 0)
    def _():
        _ffn_in_place(o_ref, h_ref, gf_ref, wgu_ref, wd_ref)

    if with_qkv:
        @pl.when(t >= 0)
        def _():
            _qkv_from(o_ref[...], gkv_ref, gq_ref, wkv_ref, wq_ref, kv_ref, q_ref)


def _pool_layer(x, g, w_in_folded, k2, v2, layer, w_out, g_ffn, w_gu, w_down, tm, qkv=None):
    b, s, d = x.shape
    d_ff = w_down.shape[0]
    tile = lambda w: pl.BlockSpec((None, tm, w), lambda i, t: (i, t, 0))
    out_shape, out_specs = jax.ShapeDtypeStruct((b, s, d), F32), tile(d)
    extra_in, extra_args = [], []
    if qkv is not None:
        g_kv, w_kv, g_q, w_q = qkv
        extra_in = [_const_spec((1, d)), _const_spec((1, d)),
                    _const_spec((d, 2 * SB_WIDTH)), _const_spec((d, MIX_WIDTH))]
        extra_args = [g_kv.reshape(1, d), g_q.reshape(1, d), w_kv, w_q]
        out_shape = (out_shape, jax.ShapeDtypeStruct((b, s, 2 * SB_WIDTH), BF16),
                     jax.ShapeDtypeStruct((b, s, MIX_WIDTH), BF16))
        out_specs = (out_specs, tile(2 * SB_WIDTH), tile(MIX_WIDTH))
    return pl.pallas_call(
        functools.partial(_pool_layer_kernel, tm=tm, with_qkv=qkv is not None),
        out_shape=out_shape,
        grid=(b, s // tm),
        in_specs=[
            tile(d),
            _const_spec((1, d)),
            _const_spec((d, MIX_WIDTH)),
            _memory_kv_spec(k2, layer), _memory_kv_spec(v2, layer),
            _const_spec((MIX_WIDTH, d)),
            _const_spec((1, d)),
            _const_spec((d, 2 * d_ff)),
            _const_spec((d_ff, d)),
        ] + extra_in,
        out_specs=out_specs,
        scratch_shapes=[pltpu.VMEM((tm + POOL_PAD, POOL_WIDTH), F32)] * 4 + [pltpu.VMEM((tm, d), BF16)],
        compiler_params=pltpu.CompilerParams(
            dimension_semantics=("arbitrary", "arbitrary"), vmem_limit_bytes=VMEM_LIMIT),
        name="pool_layer_qkv" if qkv is not None else "pool_layer",
    )(x, g.reshape(1, d), w_in_folded, k2, v2, w_out, g_ffn.reshape(1, d), w_gu, w_down, *extra_args)


def _sb_kernel(q_ref, k_ref, v_ref, tri_ref, o_ref, qpad_ref, acc_ref, c0_ref, c1_ref, *, tb):
    s = q_ref.shape[0]
    nblk = s // tb
    hb = tb // 2
    pad = tb + hb
    tri = tri_ref[...]
    rows = lax.broadcasted_iota(jnp.int32, (tb, 2 * tb), 0)
    cols = lax.broadcasted_iota(jnp.int32, (tb, 2 * tb), 1)
    before = jnp.where(cols < tb, cols, cols - tb) < rows

    def softplus2(z):
        pos = jnp.maximum(z, 0.0)
        neg = z - pos
        l = jnp.log(1.0 + jnp.exp2(neg - pos)) * LOG2_E
        return pos + l, neg - l

    def carry_lanes(c0, c1):
        n = c0.shape[0]
        return jnp.concatenate([jnp.broadcast_to(c0, (n, tb)), jnp.broadcast_to(c1, (n, tb))], axis=1)

    def row_sums(sp):
        return jnp.sum(sp[:, :tb], axis=1, keepdims=True), jnp.sum(sp[:, tb:], axis=1, keepdims=True)

    def key_block(j):
        start = pl.multiple_of(j * tb, tb)
        return (_split_heads(k_ref[pl.ds(start, tb), :], tb),
                _split_heads(v_ref[pl.ds(start, tb), :], tb))

    def unfinished(c0, c1):
        return (jnp.min(jnp.minimum(c0, c1)) < SKIP_LOG2).astype(jnp.int32)

    qpad_ref[0:s, :] = q_ref[...]
    qpad_ref[s:, :] = jnp.zeros((pad, LANES), q_ref.dtype)

    def sweep(i, carry):
        (a_c0, a_c1, a_acc), (b_c0, b_c1, b_acc), worst = carry
        j = nblk - 1 - i
        row0 = pl.multiple_of(j * tb, tb)
        k2, v2 = key_block(j)
        q = qpad_ref[pl.ds(row0, 2 * tb + hb), :]
        z = lax.dot_general(q, k2, NT_DIMS, preferred_element_type=F32)
        sp0, lb0 = softplus2(jnp.where(before, z[0:tb], -MASKED))
        sp1, lb1 = softplus2(z[tb:2 * tb])
        sp2, lb2 = softplus2(z[2 * tb:])
        later = jnp.dot(jnp.concatenate([sp0, sp1, sp2], axis=0).astype(BF16), tri,
                        preferred_element_type=F32)
        w0 = jnp.exp2(lb0 + later[0:tb])
        w1 = jnp.exp2(lb1 + later[tb:2 * tb] - carry_lanes(a_c0, a_c1))
        w2 = jnp.exp2(lb2 + later[2 * tb:] - carry_lanes(b_c0, b_c1))
        pv = jnp.dot(jnp.concatenate([w0, w1, w2], axis=0).astype(BF16), v2, preferred_element_type=F32)
        s00, s01 = row_sums(sp0)
        s10, s11 = row_sums(sp1)
        s20, s21 = row_sums(sp2)
        a_c0, a_c1, a_acc = a_c0 + s10, a_c1 + s11, a_acc + pv[tb:2 * tb]
        b_c0, b_c1, b_acc = b_c0 + s20, b_c1 + s21, b_acc + pv[2 * tb:]
        out0 = pl.multiple_of(row0 + pad, hb)
        c0 = jnp.concatenate([a_c0[hb:], b_c0], axis=0)
        c1 = jnp.concatenate([a_c1[hb:], b_c1], axis=0)
        acc_ref[pl.ds(out0, tb), :] = jnp.concatenate([a_acc[hb:], b_acc], axis=0)
        c0_ref[pl.ds(out0, tb), :] = c0
        c1_ref[pl.ds(out0, tb), :] = c1
        worst = jnp.where(j >= 1, jnp.minimum(worst, jnp.minimum(c0, c1)), worst)
        return (s00, s01, pv[0:tb]), (a_c0[:hb], a_c1[:hb], a_acc[:hb]), worst

    def padding(n):
        return jnp.full((n, 1), MASKED, F32), jnp.full((n, 1), MASKED, F32), jnp.zeros((n, LANES), F32)

    init = (padding(tb), padding(hb), jnp.full((tb, 1), SKIP_LOG2, F32))
    a, b, worst = lax.fori_loop(0, nblk, sweep, init, unroll=SB_UNROLL if nblk % SB_UNROLL == 0 else 1)
    acc_ref[0:tb, :] = a[2]
    acc_ref[tb:pad, :] = b[2]
    o_ref[...] = acc_ref[0:s, :].astype(o_ref.dtype)

    @pl.when(unfinished(worst, worst) > 0)
    def _():
        def finish(hh, _):
            row0 = pl.multiple_of(hh * hb, hb)
            q = q_ref[pl.ds(row0, hb), :]

            def one_block(carry):
                j, _, c0, c1, acc = carry
                k2, v2 = key_block(j)
                z = lax.dot_general(q, k2, NT_DIMS, preferred_element_type=F32)
                sp, log_beta = softplus2(z)
                later = jnp.dot(sp.astype(BF16), tri, preferred_element_type=F32)
                w = jnp.exp2(log_beta + later - carry_lanes(c0, c1))
                acc = acc + jnp.dot(w.astype(BF16), v2, preferred_element_type=F32)
                s0, s1 = row_sums(sp)
                return j - 1, unfinished(c0 + s0, c1 + s1), c0 + s0, c1 + s1, acc

            first_unmet = hh // 2 - 3 + hh % 2
            c0, c1 = c0_ref[pl.ds(row0, hb), :], c1_ref[pl.ds(row0, hb), :]
            carry = (first_unmet, unfinished(c0, c1), c0, c1, acc_ref[pl.ds(row0, hb), :])
            carry = lax.while_loop(lambda c: (c[0] >= 0) & (c[1] > 0), one_block, carry)
            o_ref[pl.ds(row0, hb), :] = carry[4].astype(o_ref.dtype)
            return 0

        lax.fori_loop(3, 2 * nblk, finish, 0)


def _stick_breaking(q, kv, tb):
    b, s, _ = q.shape
    pairs = SB_WIDTH // LANES
    pad = tb + tb // 2
    r = lax.broadcasted_iota(jnp.int32, (2 * tb, 2 * tb), 0)
    c = lax.broadcasted_iota(jnp.int32, (2 * tb, 2 * tb), 1)
    tri = -((r > c) & ((r < tb) == (c < tb))).astype(BF16)
    seq = lambda col: pl.BlockSpec((None, s, LANES), lambda i, p: (i, 0, col + p))
    return pl.pallas_call(
        functools.partial(_sb_kernel, tb=tb),
        out_shape=jax.ShapeDtypeStruct((b, s, SB_WIDTH), BF16),
        grid=(b, pairs),
        in_specs=[seq(0), seq(0), seq(pairs), _const_spec((2 * tb, 2 * tb))],
        out_specs=seq(0),
        scratch_shapes=[pltpu.VMEM((s + pad, LANES), BF16), pltpu.VMEM((s + pad, LANES), F32),
                        pltpu.VMEM((s + pad, 1), F32), pltpu.VMEM((s + pad, 1), F32)],
        compiler_params=pltpu.CompilerParams(
            dimension_semantics=("arbitrary", "arbitrary"), vmem_limit_bytes=VMEM_LIMIT),
        name="stick_breaking",
    )(q, kv, kv, tri)


def _sb_layer_kernel(x_ref, sb_ref, qm_ref, k2_ref, v2_ref, wout_ref, gf_ref, wgu_ref, wd_ref, gn_ref, o_ref,
                     h_ref):
    mem_out = _memory_attention(qm_ref[...], k2_ref, v2_ref)
    o_ref[...] = (x_ref[...]
                  + jnp.dot(sb_ref[...], wout_ref[0:SB_WIDTH, :], preferred_element_type=F32)
                  + jnp.dot(mem_out.astype(BF16), wout_ref[SB_WIDTH:, :], preferred_element_type=F32))

    @pl.when(pl.program_id(1) >= 0)
    def _():
        _ffn_in_place(o_ref, h_ref, gf_ref, wgu_ref, wd_ref)
        o_ref[...] = _rms_scale(o_ref[...]) * gn_ref[...]


def _sb_layer(x, sb_out, q, k2, v2, layer, w_out, g_ffn, w_gu, w_down, g_final, tm):
    b, s, d = x.shape
    d_ff = w_down.shape[0]
    tile = lambda w, col=0: pl.BlockSpec((None, tm, w), lambda i, t: (i, t, col))
    return pl.pallas_call(
        _sb_layer_kernel,
        out_shape=jax.ShapeDtypeStruct((b, s, d), F32),
        grid=(b, s // tm),
        in_specs=[
            tile(d),
            tile(SB_WIDTH),
            tile(MEM_WIDTH, SB_WIDTH // MEM_WIDTH),
            _memory_kv_spec(k2, layer), _memory_kv_spec(v2, layer),
            _const_spec((MIX_WIDTH, d)),
            _const_spec((1, d)),
            _const_spec((d, 2 * d_ff)),
            _const_spec((d_ff, d)),
            _const_spec((1, d)),
        ],
        out_specs=tile(d),
        scratch_shapes=[pltpu.VMEM((tm, d), BF16)],
        compiler_params=pltpu.CompilerParams(
            dimension_semantics=("arbitrary", "arbitrary"), vmem_limit_bytes=VMEM_LIMIT),
        name="sb_layer",
    )(x, sb_out, q, k2, v2, w_out, g_ffn.reshape(1, d), w_gu, w_down, g_final.reshape(1, d))


def _block_diag(w_group):
    g, c, _ = w_group.shape
    eye = jnp.eye(g, dtype=w_group.dtype)
    return jnp.einsum("gcd,gh->gchd", w_group, eye).reshape(g * c, g * c)


def kernel(x, mem, mem_norm, a_norm_mix, a_w_in, a_w_group, a_scale, a_w_mem_kv, a_w_out, a_norm_ffn,
           a_w_gu, a_w_down, kv_norm, w_kv, b_norm_mix, b_w_q, b_w_mem_kv, b_w_out, b_norm_ffn, b_w_gu,
           b_w_down, final_norm):
    b, s, d = x.shape
    na, nb = a_w_in.shape[0], b_w_q.shape[0]
    assert na >= 1 and nb == 1, "pooling layers followed by exactly one stick-breaking layer"
    tm = min(ROW_TILE, s)

    k2, v2 = _memory_kv(mem, mem_norm, jnp.concatenate([a_w_mem_kv, b_w_mem_kv], axis=0).astype(BF16))

    for i in range(na - 1):
        w_in_folded = _fold_pool_weights(a_w_in[i], _block_diag(a_w_group[i]), a_scale[i])
        x = _pool_layer(x, a_norm_mix[i], w_in_folded, k2, v2, i, a_w_out[i].astype(BF16), a_norm_ffn[i],
                        a_w_gu[i].astype(BF16), a_w_down[i].astype(BF16), tm)
    i = na - 1
    w_in_folded = _fold_pool_weights(a_w_in[i], _block_diag(a_w_group[i]), a_scale[i])
    x, kv, q = _pool_layer(x, a_norm_mix[i], w_in_folded, k2, v2, i, a_w_out[i].astype(BF16), a_norm_ffn[i],
                           a_w_gu[i].astype(BF16), a_w_down[i].astype(BF16), tm,
                           qkv=(kv_norm, w_kv.astype(BF16), b_norm_mix[0], b_w_q[0].astype(BF16)))

    sb_out = _stick_breaking(q, kv, SB_TILE)
    return _sb_layer(x, sb_out, q, k2, v2, na, b_w_out[0].astype(BF16), b_norm_ffn[0],
                     b_w_gu[0].astype(BF16), b_w_down[0].astype(BF16), final_norm, tm)
```

```python
import functools

import jax
import jax.numpy as jnp
from jax import lax
from jax.experimental import pallas as pl
from jax.experimental.pallas import tpu as pltpu

HEAD_DIM = 64
N_SB_HEADS = 12
SB_WIDTH = N_SB_HEADS * HEAD_DIM
N_MEM_HEADS = 4
MEM_WIDTH = N_MEM_HEADS * HEAD_DIM
MIX_WIDTH = SB_WIDTH + MEM_WIDTH
POOL_WINDOWS = (2, 4, 8, 16)
POOL_WIDTH = SB_WIDTH
POOL_GROUP = POOL_WIDTH // len(POOL_WINDOWS)
EPS = 1e-6
LOG2_E = 1.4426950408889634
QK_SCALE_LOG2 = HEAD_DIM ** -0.5 * LOG2_E

LANES = 128
ROW_TILE = 512
SB_LAYER_ROW_TILE = 1024
BF16_ROWS = 16
POOL_PAD = 32
FF_CHUNK = 256
SB_TILE = 128
SB_UNROLL = 32
SKIP_LOG2 = 151.0
MASKED = 1e4
VMEM_LIMIT = 56 * 1024 * 1024

F32 = jnp.float32
BF16 = jnp.bfloat16
NT_DIMS = (((1,), (1,)), ((), ()))


def _const_spec(shape):
    nd = len(shape)
    return pl.BlockSpec(shape, lambda *_: (0,) * nd, pipeline_mode=pl.Buffered(1))


def _rms_scale(x):
    return x * lax.rsqrt(jnp.mean(x * x, axis=-1, keepdims=True) + EPS)


def _split_heads(t, rows):
    lane = lax.broadcasted_iota(jnp.int32, (rows, LANES), 1)
    first = lane < HEAD_DIM
    zero = jnp.zeros_like(t)
    return jnp.concatenate([jnp.where(first, t, zero), jnp.where(first, zero, t)], axis=0)


def _memkv_kernel(mem_ref, g_ref, w_ref, k2_ref, v2_ref, *, m):
    h = (_rms_scale(mem_ref[...]) * g_ref[...]).astype(BF16)
    kv = jnp.dot(h, w_ref[...], preferred_element_type=F32)
    for i in range(k2_ref.shape[0]):
        for p in range(MEM_WIDTH // LANES):
            k = kv[i * m:(i + 1) * m, p * LANES:(p + 1) * LANES]
            v = kv[i * m:(i + 1) * m, MEM_WIDTH + p * LANES:MEM_WIDTH + (p + 1) * LANES]
            k2_ref[i, p] = _split_heads(k, m).astype(BF16)
            v2_ref[i, p] = _split_heads(v, m).astype(BF16)


def _memory_kv(mem, mem_norm, w_mem_kv):
    b, m, d = mem.shape
    n_layers = w_mem_kv.shape[0]
    pairs = MEM_WIDTH // LANES
    out = jax.ShapeDtypeStruct((n_layers, b, pairs, 2 * m, LANES), BF16)
    out_spec = pl.BlockSpec((None, b, pairs, 2 * m, LANES), lambda l: (l, 0, 0, 0, 0))
    return pl.pallas_call(
        functools.partial(_memkv_kernel, m=m),
        out_shape=(out, out),
        grid=(n_layers,),
        in_specs=[
            _const_spec((b * m, d)),
            _const_spec((1, d)),
            pl.BlockSpec((None, d, 2 * MEM_WIDTH), lambda l: (l, 0, 0)),
        ],
        out_specs=(out_spec, out_spec),
        compiler_params=pltpu.CompilerParams(dimension_semantics=("arbitrary",), vmem_limit_bytes=VMEM_LIMIT),
        name="memory_kv",
    )(mem.reshape(b * m, d), mem_norm.reshape(1, d), w_mem_kv)


def _memory_kv_spec(k2, layer):
    return pl.BlockSpec((None, None) + k2.shape[2:], lambda i, t: (layer, i, 0, 0, 0))


def _memory_attention(qm, k2_ref, v2_ref):
    t = qm.shape[0]
    m = k2_ref.shape[1] // 2
    lane = lax.broadcasted_iota(jnp.int32, (t, LANES), 1)
    outs = []
    for p in range(MEM_WIDTH // LANES):
        logits = lax.dot_general(qm[:, p * LANES:(p + 1) * LANES], k2_ref[p], NT_DIMS,
                                 preferred_element_type=F32)
        probs, inv = [], []
        for hh in range(2):
            lg = logits[:, hh * m:(hh + 1) * m]
            pe = jnp.exp2(lg - jnp.max(lg, axis=1, keepdims=True))
            probs.append(pe)
            inv.append(1.0 / jnp.sum(pe, axis=1, keepdims=True))
        pv = jnp.dot(jnp.concatenate(probs, axis=1).astype(BF16), v2_ref[p],
                     preferred_element_type=F32)
        outs.append(pv * jnp.where(lane < HEAD_DIM, inv[0], inv[1]))
    return jnp.concatenate(outs, axis=1)


def _ffn_in_place(o_ref, h_ref, g_ref, wgu_ref, wd_ref):
    h_ref[...] = (_rms_scale(o_ref[...]) * g_ref[...]).astype(BF16)
    d_ff = wd_ref.shape[0]
    for c in range(d_ff // FF_CHUNK):
        lo, hi = c * FF_CHUNK, (c + 1) * FF_CHUNK
        h = h_ref[...]
        gate = jnp.dot(h, wgu_ref[:, lo:hi], preferred_element_type=F32)
        up = jnp.dot(h, wgu_ref[:, d_ff + lo:d_ff + hi], preferred_element_type=F32)
        act = gate * (1.0 / (1.0 + jnp.exp(-gate))) * up
        o_ref[...] += jnp.dot(act.astype(BF16), wd_ref[lo:hi, :], preferred_element_type=F32)


def _fold_kernel(win_ref, wgrp_ref, scale_ref, o_ref):
    a = win_ref[:, :POOL_WIDTH]
    b = wgrp_ref[...]
    a_hi, b_hi = a.astype(BF16), b.astype(BF16)
    a_lo, b_lo = (a - a_hi.astype(F32)).astype(BF16), (b - b_hi.astype(F32)).astype(BF16)
    prod = (jnp.dot(a_hi, b_hi, preferred_element_type=F32) + jnp.dot(a_hi, b_lo, preferred_element_type=F32)
            + jnp.dot(a_lo, b_hi, preferred_element_type=F32))
    o_ref[:, :POOL_WIDTH] = (prod * scale_ref[...]).astype(BF16)
    o_ref[:, POOL_WIDTH:] = win_ref[:, POOL_WIDTH:].astype(BF16)


def _fold_pool_weights(w_in, w_group_full, scale):
    d = w_in.shape[0]
    tr = min(256, d)
    return pl.pallas_call(
        _fold_kernel,
        out_shape=jax.ShapeDtypeStruct((d, MIX_WIDTH), BF16),
        grid=(d // tr,),
        in_specs=[pl.BlockSpec((tr, MIX_WIDTH), lambda i: (i, 0)),
                  _const_spec((POOL_WIDTH, POOL_WIDTH)), _const_spec((1, POOL_WIDTH))],
        out_specs=pl.BlockSpec((tr, MIX_WIDTH), lambda i: (i, 0)),
        name="fold_pool_weights",
    )(w_in, w_group_full, scale.reshape(1, POOL_WIDTH))


def _qkv_from(x, gkv_ref, gq_ref, wkv_ref, wq_ref, kv_ref, q_ref):
    xn = _rms_scale(x)
    kv = jnp.dot((xn * gkv_ref[...]).astype(BF16), wkv_ref[...], preferred_element_type=F32)
    kv_ref[...] = kv.astype(BF16)
    q = jnp.dot((xn * gq_ref[...]).astype(BF16), wq_ref[...], preferred_element_type=F32)
    q_ref[...] = (q * QK_SCALE_LOG2).astype(BF16)


def _pool_layer_kernel(x_ref, g_ref, win_ref, k2_ref, v2_ref, wout_ref, gf_ref, wgu_ref, wd_ref, *rest,
                       tm, with_qkv):
    if with_qkv:
        gkv_ref, gq_ref, wkv_ref, wq_ref, o_ref, kv_ref, q_ref, e_ref, s2_ref, s4_ref, s8_ref, h_ref = rest
    else:
        o_ref, e_ref, s2_ref, s4_ref, s8_ref, h_ref = rest
    t = pl.program_id(1)
    x = x_ref[...]
    h = (_rms_scale(x) * g_ref[...]).astype(BF16)
    proj = jnp.dot(h, win_ref[...], preferred_element_type=F32)
    u = proj[:, :POOL_WIDTH]

    @pl.when(t == 0)
    def _():
        e_ref[0:POOL_PAD, :] = jnp.zeros((POOL_PAD, POOL_WIDTH), F32)

    @pl.when(t > 0)
    def _():
        e_ref[POOL_PAD - 16:POOL_PAD, :] = e_ref[tm + POOL_PAD - 16:tm + POOL_PAD, :]

    e_ref[POOL_PAD:, :] = u
    n = tm + POOL_PAD
    s2_ref[8:, :] = e_ref[8:, :] + e_ref[7:n - 1, :]
    s4_ref[16:, 128:] = s2_ref[16:, 128:] + s2_ref[14:n - 2, 128:]
    s8_ref[24:, 384:] = s4_ref[24:, 384:] + s4_ref[20:n - 4, 384:]
    s16 = s8_ref[32:, 512:] + s8_ref[24:n - 8, 512:]

    pos1 = t * tm + lax.broadcasted_iota(jnp.int32, (tm, 1), 0) + 1
    inv = [1.0 / jnp.minimum(pos1, w).astype(F32) for w in POOL_WINDOWS]
    lane = lax.broadcasted_iota(jnp.int32, (tm, LANES), 1)
    low = lane < (POOL_GROUP - LANES)
    grouped = jnp.concatenate([
        s2_ref[POOL_PAD:, 0:128] * inv[0],
        jnp.where(low, s2_ref[POOL_PAD:, 128:256] * inv[0], s4_ref[POOL_PAD:, 128:256] * inv[1]),
        s4_ref[POOL_PAD:, 256:384] * inv[1],
        s8_ref[POOL_PAD:, 384:512] * inv[2],
        jnp.where(low, s8_ref[POOL_PAD:, 512:640] * inv[2], s16[:, 0:128] * inv[3]),
        s16[:, 128:256] * inv[3],
    ], axis=1) - u

    qm = (proj[:, POOL_WIDTH:] * QK_SCALE_LOG2).astype(BF16)
    mem_out = _memory_attention(qm, k2_ref, v2_ref)
    o_ref[...] = (x
                  + jnp.dot(grouped.astype(BF16), wout_ref[0:POOL_WIDTH, :], preferred_element_type=F32)
                  + jnp.dot(mem_out.astype(BF16), wout_ref[POOL_WIDTH:, :], preferred_element_type=F32))

    @pl.when(t >= 0)
    def _():
        _ffn_in_place(o_ref, h_ref, gf_ref, wgu_ref, wd_ref)

    if with_qkv:
        @pl.when(t >= 0)
        def _():
            _qkv_from(o_ref[...], gkv_ref, gq_ref, wkv_ref, wq_ref, kv_ref, q_ref)


def _pool_layer(x, g, w_in_folded, k2, v2, layer, w_out, g_ffn, w_gu, w_down, tm, qkv=None):
    b, s, d = x.shape
    d_ff = w_down.shape[0]
    tile = lambda w: pl.BlockSpec((None, tm, w), lambda i, t: (i, t, 0))
    out_shape, out_specs = jax.ShapeDtypeStruct((b, s, d), F32), tile(d)
    extra_in, extra_args = [], []
    if qkv is not None:
        g_kv, w_kv, g_q, w_q = qkv
        extra_in = [_const_spec((1, d)), _const_spec((1, d)),
                    _const_spec((d, 2 * SB_WIDTH)), _const_spec((d, MIX_WIDTH))]
        extra_args = [g_kv.reshape(1, d), g_q.reshape(1, d), w_kv, w_q]
        out_shape = (out_shape, jax.ShapeDtypeStruct((b, s, 2 * SB_WIDTH), BF16),
                     jax.ShapeDtypeStruct((b, s, MIX_WIDTH), BF16))
        out_specs = (out_specs, tile(2 * SB_WIDTH), tile(MIX_WIDTH))
    return pl.pallas_call(
        functools.partial(_pool_layer_kernel, tm=tm, with_qkv=qkv is not None),
        out_shape=out_shape,
        grid=(b, s // tm),
        in_specs=[
            tile(d),
            _const_spec((1, d)),
            _const_spec((d, MIX_WIDTH)),
            _memory_kv_spec(k2, layer), _memory_kv_spec(v2, layer),
            _const_spec((MIX_WIDTH, d)),
            _const_spec((1, d)),
            _const_spec((d, 2 * d_ff)),
            _const_spec((d_ff, d)),
        ] + extra_in,
        out_specs=out_specs,
        scratch_shapes=[pltpu.VMEM((tm + POOL_PAD, POOL_WIDTH), F32)] * 4 + [pltpu.VMEM((tm, d), BF16)],
        compiler_params=pltpu.CompilerParams(
            dimension_semantics=("arbitrary", "arbitrary"), vmem_limit_bytes=VMEM_LIMIT),
        name="pool_layer_qkv" if qkv is not None else "pool_layer",
    )(x, g.reshape(1, d), w_in_folded, k2, v2, w_out, g_ffn.reshape(1, d), w_gu, w_down, *extra_args)


def _sb_kernel(q_ref, k_ref, v_ref, tri_ref, *rest, tb, n_cast):
    cast_in, (o_ref, *cast_out), scratch = rest[:n_cast], rest[n_cast:2 * n_cast + 1], rest[2 * n_cast + 1:]
    qpad_ref, acc_ref, c0_ref, c1_ref = scratch
    for src, dst in zip(cast_in, cast_out):
        dst[...] = src[...].astype(dst.dtype)

    s = q_ref.shape[0]
    nblk = s // tb
    hb = tb // 2
    pad = tb + hb
    tri = tri_ref[...]
    rows = lax.broadcasted_iota(jnp.int32, (tb, 2 * tb), 0)
    cols = lax.broadcasted_iota(jnp.int32, (tb, 2 * tb), 1)
    before = jnp.where(cols < tb, cols, cols - tb) < rows

    def softplus2(z):
        pos = jnp.maximum(z, 0.0)
        neg = z - pos
        l = jnp.log(1.0 + jnp.exp2(neg - pos)) * LOG2_E
        return pos + l, neg - l

    def carry_lanes(c0, c1):
        n = c0.shape[0]
        return jnp.concatenate([jnp.broadcast_to(c0, (n, tb)), jnp.broadcast_to(c1, (n, tb))], axis=1)

    def row_sums(sp):
        return jnp.sum(sp[:, :tb], axis=1, keepdims=True), jnp.sum(sp[:, tb:], axis=1, keepdims=True)

    def key_block(j):
        start = pl.multiple_of(j * tb, tb)
        return (_split_heads(k_ref[pl.ds(start, tb), :], tb),
                _split_heads(v_ref[pl.ds(start, tb), :], tb))

    def unfinished(c0, c1):
        return (jnp.min(jnp.minimum(c0, c1)) < SKIP_LOG2).astype(jnp.int32)

    qpad_ref[0:s, :] = q_ref[...]
    qpad_ref[s:, :] = jnp.zeros((pad, LANES), q_ref.dtype)

    def sweep(i, carry):
        (a_c0, a_c1, a_acc), (b_c0, b_c1, b_acc), worst = carry
        j = nblk - 1 - i
        row0 = pl.multiple_of(j * tb, tb)
        k2, v2 = key_block(j)
        q = qpad_ref[pl.ds(row0, 2 * tb + hb), :]
        z = lax.dot_general(q, k2, NT_DIMS, preferred_element_type=F32)
        sp0, lb0 = softplus2(jnp.where(before, z[0:tb], -MASKED))
        sp1, lb1 = softplus2(z[tb:2 * tb])
        sp2, lb2 = softplus2(z[2 * tb:])
        later = jnp.dot(jnp.concatenate([sp0, sp1, sp2], axis=0).astype(BF16), tri,
                        preferred_element_type=F32)
        w0 = jnp.exp2(lb0 + later[0:tb])
        w1 = jnp.exp2(lb1 + later[tb:2 * tb] - carry_lanes(a_c0, a_c1))
        w2 = jnp.exp2(lb2 + later[2 * tb:] - carry_lanes(b_c0, b_c1))
        pv = jnp.dot(jnp.concatenate([w0, w1, w2], axis=0).astype(BF16), v2, preferred_element_type=F32)
        s00, s01 = row_sums(sp0)
        s10, s11 = row_sums(sp1)
        s20, s21 = row_sums(sp2)
        a_c0, a_c1, a_acc = a_c0 + s10, a_c1 + s11, a_acc + pv[tb:2 * tb]
        b_c0, b_c1, b_acc = b_c0 + s20, b_c1 + s21, b_acc + pv[2 * tb:]
        out0 = pl.multiple_of(row0 + pad, hb)
        c0 = jnp.concatenate([a_c0[hb:], b_c0], axis=0)
        c1 = jnp.concatenate([a_c1[hb:], b_c1], axis=0)
        acc_ref[pl.ds(out0, tb), :] = jnp.concatenate([a_acc[hb:], b_acc], axis=0)
        c0_ref[pl.ds(out0, tb), :] = c0
        c1_ref[pl.ds(out0, tb), :] = c1
        worst = jnp.where(j >= 1, jnp.minimum(worst, jnp.minimum(c0, c1)), worst)
        return (s00, s01, pv[0:tb]), (a_c0[:hb], a_c1[:hb], a_acc[:hb]), worst

    def padding(n):
        return jnp.full((n, 1), MASKED, F32), jnp.full((n, 1), MASKED, F32), jnp.zeros((n, LANES), F32)

    init = (padding(tb), padding(hb), jnp.full((tb, 1), SKIP_LOG2, F32))
    a, b, worst = lax.fori_loop(0, nblk, sweep, init, unroll=SB_UNROLL if nblk % SB_UNROLL == 0 else 1)
    acc_ref[0:tb, :] = a[2]
    acc_ref[tb:pad, :] = b[2]
    o_ref[...] = acc_ref[0:s, :].astype(o_ref.dtype)

    @pl.when(unfinished(worst, worst) > 0)
    def _():
        def finish(hh, _):
            row0 = pl.multiple_of(hh * hb, hb)
            q = q_ref[pl.ds(row0, hb), :]

            def one_block(carry):
                j, _, c0, c1, acc = carry
                k2, v2 = key_block(j)
                z = lax.dot_general(q, k2, NT_DIMS, preferred_element_type=F32)
                sp, log_beta = softplus2(z)
                later = jnp.dot(sp.astype(BF16), tri, preferred_element_type=F32)
                w = jnp.exp2(log_beta + later - carry_lanes(c0, c1))
                acc = acc + jnp.dot(w.astype(BF16), v2, preferred_element_type=F32)
                s0, s1 = row_sums(sp)
                return j - 1, unfinished(c0 + s0, c1 + s1), c0 + s0, c1 + s1, acc

            first_unmet = hh // 2 - 3 + hh % 2
            c0, c1 = c0_ref[pl.ds(row0, hb), :], c1_ref[pl.ds(row0, hb), :]
            carry = (first_unmet, unfinished(c0, c1), c0, c1, acc_ref[pl.ds(row0, hb), :])
            carry = lax.while_loop(lambda c: (c[0] >= 0) & (c[1] > 0), one_block, carry)
            o_ref[pl.ds(row0, hb), :] = carry[4].astype(o_ref.dtype)
            return 0

        lax.fori_loop(3, 2 * nblk, finish, 0)


def _row_blocks(rows, steps):
    for rb in range(BF16_ROWS, rows + 1, BF16_ROWS):
        if rows % rb == 0 and rows // rb <= steps:
            return rb
    raise ValueError(f"cannot split {rows} rows over {steps} steps")


def _stick_breaking(q, kv, tb, to_cast):
    b, s, _ = q.shape
    pairs = SB_WIDTH // LANES
    cast_specs = []
    for w in to_cast:
        rb = _row_blocks(w.shape[0], b * pairs)
        last = w.shape[0] // rb - 1
        cast_specs.append(pl.BlockSpec((rb, w.shape[1]), lambda i, p, last=last: (jnp.minimum(i * pairs + p, last), 0)))
    pad = tb + tb // 2
    r = lax.broadcasted_iota(jnp.int32, (2 * tb, 2 * tb), 0)
    c = lax.broadcasted_iota(jnp.int32, (2 * tb, 2 * tb), 1)
    tri = -((r > c) & ((r < tb) == (c < tb))).astype(BF16)
    seq = lambda col: pl.BlockSpec((None, s, LANES), lambda i, p: (i, 0, col + p))
    return pl.pallas_call(
        functools.partial(_sb_kernel, tb=tb, n_cast=len(to_cast)),
        out_shape=(jax.ShapeDtypeStruct((b, s, SB_WIDTH), BF16),
                   *[jax.ShapeDtypeStruct(w.shape, BF16) for w in to_cast]),
        grid=(b, pairs),
        in_specs=[seq(0), seq(0), seq(pairs), _const_spec((2 * tb, 2 * tb)), *cast_specs],
        out_specs=(seq(0), *cast_specs),
        scratch_shapes=[pltpu.VMEM((s + pad, LANES), BF16), pltpu.VMEM((s + pad, LANES), F32),
                        pltpu.VMEM((s + pad, 1), F32), pltpu.VMEM((s + pad, 1), F32)],
        compiler_params=pltpu.CompilerParams(
            dimension_semantics=("arbitrary", "arbitrary"), vmem_limit_bytes=VMEM_LIMIT),
        name="stick_breaking",
    )(q, kv, kv, tri, *to_cast)


def _sb_layer_kernel(x_ref, sb_ref, qm_ref, k2_ref, v2_ref, wout_ref, gf_ref, wgu_ref, wd_ref, gn_ref, o_ref,
                     h_ref):
    mem_out = _memory_attention(qm_ref[...], k2_ref, v2_ref)
    o_ref[...] = (x_ref[...]
                  + jnp.dot(sb_ref[...], wout_ref[0:SB_WIDTH, :], preferred_element_type=F32)
                  + jnp.dot(mem_out.astype(BF16), wout_ref[SB_WIDTH:, :], preferred_element_type=F32))

    @pl.when(pl.program_id(1) >= 0)
    def _():
        _ffn_in_place(o_ref, h_ref, gf_ref, wgu_ref, wd_ref)
        o_ref[...] = _rms_scale(o_ref[...]) * gn_ref[...]


def _sb_layer(x, sb_out, q, k2, v2, layer, w_out, g_ffn, w_gu, w_down, g_final, tm):
    b, s, d = x.shape
    d_ff = w_down.shape[0]
    tile = lambda w, col=0: pl.BlockSpec((None, tm, w), lambda i, t: (i, t, col))
    return pl.pallas_call(
        _sb_layer_kernel,
        out_shape=jax.ShapeDtypeStruct((b, s, d), F32),
        grid=(b, s // tm),
        in_specs=[
            tile(d),
            tile(SB_WIDTH),
            tile(MEM_WIDTH, SB_WIDTH // MEM_WIDTH),
            _memory_kv_spec(k2, layer), _memory_kv_spec(v2, layer),
            _const_spec((MIX_WIDTH, d)),
            _const_spec((1, d)),
            _const_spec((d, 2 * d_ff)),
            _const_spec((d_ff, d)),
            _const_spec((1, d)),
        ],
        out_specs=tile(d),
        scratch_shapes=[pltpu.VMEM((tm, d), BF16)],
        compiler_params=pltpu.CompilerParams(
            dimension_semantics=("arbitrary", "arbitrary"), vmem_limit_bytes=VMEM_LIMIT),
        name="sb_layer",
    )(x, sb_out, q, k2, v2, w_out, g_ffn.reshape(1, d), w_gu, w_down, g_final.reshape(1, d))


def _block_diag(w_group):
    g, c, _ = w_group.shape
    eye = jnp.eye(g, dtype=w_group.dtype)
    return jnp.einsum("gcd,gh->gchd", w_group, eye).reshape(g * c, g * c)


def kernel(x, mem, mem_norm, a_norm_mix, a_w_in, a_w_group, a_scale, a_w_mem_kv, a_w_out, a_norm_ffn,
           a_w_gu, a_w_down, kv_norm, w_kv, b_norm_mix, b_w_q, b_w_mem_kv, b_w_out, b_norm_ffn, b_w_gu,
           b_w_down, final_norm):
    b, s, d = x.shape
    na, nb = a_w_in.shape[0], b_w_q.shape[0]
    assert na >= 1 and nb == 1, "pooling layers followed by exactly one stick-breaking layer"
    tm, tm_sb = min(ROW_TILE, s), min(SB_LAYER_ROW_TILE, s)
    assert s % tm == 0 and s % tm_sb == 0 and s % SB_TILE == 0, "sequence length must be a multiple of the row tiles"

    k2, v2 = _memory_kv(mem, mem_norm, jnp.concatenate([a_w_mem_kv, b_w_mem_kv], axis=0).astype(BF16))

    for i in range(na - 1):
        w_in_folded = _fold_pool_weights(a_w_in[i], _block_diag(a_w_group[i]), a_scale[i])
        x = _pool_layer(x, a_norm_mix[i], w_in_folded, k2, v2, i, a_w_out[i].astype(BF16), a_norm_ffn[i],
                        a_w_gu[i].astype(BF16), a_w_down[i].astype(BF16), tm)
    i = na - 1
    w_in_folded = _fold_pool_weights(a_w_in[i], _block_diag(a_w_group[i]), a_scale[i])
    x, kv, q = _pool_layer(x, a_norm_mix[i], w_in_folded, k2, v2, i, a_w_out[i].astype(BF16), a_norm_ffn[i],
                           a_w_gu[i].astype(BF16), a_w_down[i].astype(BF16), tm,
                           qkv=(kv_norm, w_kv.astype(BF16), b_norm_mix[0], b_w_q[0].astype(BF16)))

    sb_out, w_out, w_gu, w_down = _stick_breaking(q, kv, SB_TILE, (b_w_out[0], b_w_gu[0], b_w_down[0]))
    return _sb_layer(x, sb_out, q, k2, v2, na, w_out, b_norm_ffn[0], w_gu, w_down, final_norm, tm_sb)
```

```python
import functools

import jax
import jax.numpy as jnp
from jax import lax
from jax.experimental import pallas as pl
from jax.experimental.pallas import tpu as pltpu

HEAD_DIM = 64
N_SB_HEADS = 12
SB_WIDTH = N_SB_HEADS * HEAD_DIM
N_MEM_HEADS = 4
MEM_WIDTH = N_MEM_HEADS * HEAD_DIM
MIX_WIDTH = SB_WIDTH + MEM_WIDTH
POOL_WINDOWS = (2, 4, 8, 16)
POOL_WIDTH = SB_WIDTH
POOL_GROUP = POOL_WIDTH // len(POOL_WINDOWS)
EPS = 1e-6
LOG2_E = 1.4426950408889634
QK_SCALE_LOG2 = HEAD_DIM ** -0.5 * LOG2_E

LANES = 128
ROW_TILE = 512
SB_LAYER_ROW_TILE = 1024
BF16_ROWS = 16
FOLD_ROWS = 128
POOL_PAD = 32
FF_CHUNK = 256
SB_TILE = 128
SB_UNROLL = 32
SKIP_LOG2 = 151.0
EXP2_MAX = 126.0
MASKED = 1e4
VMEM_LIMIT = 56 * 1024 * 1024

F32 = jnp.float32
BF16 = jnp.bfloat16
NT_DIMS = (((1,), (1,)), ((), ()))


def _const_spec(shape):
    nd = len(shape)
    return pl.BlockSpec(shape, lambda *_: (0,) * nd, pipeline_mode=pl.Buffered(1))


def _rms_scale(x):
    return x * lax.rsqrt(jnp.mean(x * x, axis=-1, keepdims=True) + EPS)


def _split_heads(t, rows):
    lane = lax.broadcasted_iota(jnp.int32, (rows, LANES), 1)
    first = lane < HEAD_DIM
    zero = jnp.zeros_like(t)
    return jnp.concatenate([jnp.where(first, t, zero), jnp.where(first, zero, t)], axis=0)


def _row_blocks(rows, steps):
    for rb in range(BF16_ROWS, rows + 1, BF16_ROWS):
        if rows % rb == 0 and rows // rb <= steps:
            return rb
    raise ValueError(f"cannot split {rows} rows over {steps} steps")


def _cast_rider_specs(to_cast, steps, step_of):
    specs = []
    for w in to_cast:
        rb = _row_blocks(w.shape[0], steps)
        last = w.shape[0] // rb - 1
        specs.append(pl.BlockSpec((rb, w.shape[1]), lambda *g, last=last: (jnp.minimum(step_of(*g), last), 0)))
    return specs


def _cast_riders(cast_in, cast_out):
    for src, dst in zip(cast_in, cast_out):
        dst[...] = src[...].astype(dst.dtype)


def _memkv_kernel(mem_ref, g_ref, w_ref, k2_ref, v2_ref, *, m):
    h = (_rms_scale(mem_ref[...]) * g_ref[...]).astype(BF16)
    kv = jnp.dot(h, w_ref[...], preferred_element_type=F32)
    for i in range(k2_ref.shape[0]):
        for p in range(MEM_WIDTH // LANES):
            k = kv[i * m:(i + 1) * m, p * LANES:(p + 1) * LANES]
            v = kv[i * m:(i + 1) * m, MEM_WIDTH + p * LANES:MEM_WIDTH + (p + 1) * LANES]
            k2_ref[i, p] = _split_heads(k, m).astype(BF16)
            v2_ref[i, p] = _split_heads(v, m).astype(BF16)


def _memory_kv(mem, mem_norm, w_mem_kv):
    b, m, d = mem.shape
    n_layers = w_mem_kv.shape[0]
    pairs = MEM_WIDTH // LANES
    out = jax.ShapeDtypeStruct((n_layers, b, pairs, 2 * m, LANES), BF16)
    out_spec = pl.BlockSpec((None, b, pairs, 2 * m, LANES), lambda l: (l, 0, 0, 0, 0))
    return pl.pallas_call(
        functools.partial(_memkv_kernel, m=m),
        out_shape=(out, out),
        grid=(n_layers,),
        in_specs=[
            _const_spec((b * m, d)),
            _const_spec((1, d)),
            pl.BlockSpec((None, d, 2 * MEM_WIDTH), lambda l: (l, 0, 0)),
        ],
        out_specs=(out_spec, out_spec),
        compiler_params=pltpu.CompilerParams(dimension_semantics=("arbitrary",), vmem_limit_bytes=VMEM_LIMIT),
        name="memory_kv",
    )(mem.reshape(b * m, d), mem_norm.reshape(1, d), w_mem_kv)


def _memory_kv_spec(k2, layer):
    return pl.BlockSpec((None, None) + k2.shape[2:], lambda i, t: (layer, i, 0, 0, 0))


def _memory_attention(qm, k2_ref, v2_ref):
    t = qm.shape[0]
    m = k2_ref.shape[1] // 2
    lane = lax.broadcasted_iota(jnp.int32, (t, LANES), 1)
    outs = []
    for p in range(MEM_WIDTH // LANES):
        logits = lax.dot_general(qm[:, p * LANES:(p + 1) * LANES], k2_ref[p], NT_DIMS,
                                 preferred_element_type=F32)
        probs, inv = [], []
        for hh in range(2):
            lg = logits[:, hh * m:(hh + 1) * m]
            pe = jnp.exp2(lg - jnp.max(lg, axis=1, keepdims=True))
            probs.append(pe)
            inv.append(1.0 / jnp.sum(pe, axis=1, keepdims=True))
        pv = jnp.dot(jnp.concatenate(probs, axis=1).astype(BF16), v2_ref[p],
                     preferred_element_type=F32)
        outs.append(pv * jnp.where(lane < HEAD_DIM, inv[0], inv[1]))
    return jnp.concatenate(outs, axis=1)


def _ffn_in_place(o_ref, h_ref, g_ref, wgu_ref, wd_ref):
    h_ref[...] = (_rms_scale(o_ref[...]) * g_ref[...]).astype(BF16)
    d_ff = wd_ref.shape[0]
    for c in range(d_ff // FF_CHUNK):
        lo, hi = c * FF_CHUNK, (c + 1) * FF_CHUNK
        h = h_ref[...]
        gate = jnp.dot(h, wgu_ref[:, lo:hi], preferred_element_type=F32)
        up = jnp.dot(h, wgu_ref[:, d_ff + lo:d_ff + hi], preferred_element_type=F32)
        act = gate * (1.0 / (1.0 + jnp.exp(-gate))) * up
        o_ref[...] += jnp.dot(act.astype(BF16), wd_ref[lo:hi, :], preferred_element_type=F32)


def _fold_kernel(win_ref, wgrp_ref, scale_ref, *rest, n_cast):
    o_ref = rest[n_cast]
    _cast_riders(rest[:n_cast], rest[n_cast + 1:])
    a = win_ref[:, :POOL_WIDTH]
    b = wgrp_ref[...]
    a_hi, b_hi = a.astype(BF16), b.astype(BF16)
    a_lo, b_lo = (a - a_hi.astype(F32)).astype(BF16), (b - b_hi.astype(F32)).astype(BF16)
    prod = (jnp.dot(a_hi, b_hi, preferred_element_type=F32) + jnp.dot(a_hi, b_lo, preferred_element_type=F32)
            + jnp.dot(a_lo, b_hi, preferred_element_type=F32))
    o_ref[:, :POOL_WIDTH] = (prod * scale_ref[...]).astype(BF16)
    o_ref[:, POOL_WIDTH:] = win_ref[:, POOL_WIDTH:].astype(BF16)


def _fold_pool_weights(w_in, w_group_full, scale, to_cast):
    d = w_in.shape[0]
    tr = min(FOLD_ROWS, d)
    rider_specs = _cast_rider_specs(to_cast, d // tr, lambda i: i)
    folded_spec = pl.BlockSpec((tr, MIX_WIDTH), lambda i: (i, 0))
    return pl.pallas_call(
        functools.partial(_fold_kernel, n_cast=len(to_cast)),
        out_shape=(jax.ShapeDtypeStruct((d, MIX_WIDTH), BF16), *[jax.ShapeDtypeStruct(w.shape, BF16) for w in to_cast]),
        grid=(d // tr,),
        in_specs=[folded_spec, _const_spec((POOL_WIDTH, POOL_WIDTH)), _const_spec((1, POOL_WIDTH)), *rider_specs],
        out_specs=(folded_spec, *rider_specs),
        compiler_params=pltpu.CompilerParams(dimension_semantics=("arbitrary",), vmem_limit_bytes=VMEM_LIMIT),
        name="fold_pool_weights",
    )(w_in, w_group_full, scale.reshape(1, POOL_WIDTH), *to_cast)


def _qkv_from(x, gkv_ref, gq_ref, wkv_ref, wq_ref, kv_ref, q_ref):
    xn = _rms_scale(x)
    kv = jnp.dot((xn * gkv_ref[...]).astype(BF16), wkv_ref[...], preferred_element_type=F32)
    kv_ref[...] = kv.astype(BF16)
    q = jnp.dot((xn * gq_ref[...]).astype(BF16), wq_ref[...], preferred_element_type=F32)
    q_ref[...] = (q * QK_SCALE_LOG2).astype(BF16)


def _pool_layer_kernel(x_ref, g_ref, win_ref, k2_ref, v2_ref, wout_ref, gf_ref, wgu_ref, wd_ref, *rest,
                       tm, with_qkv):
    if with_qkv:
        gkv_ref, gq_ref, wkv_ref, wq_ref, o_ref, kv_ref, q_ref, e_ref, s2_ref, s4_ref, s8_ref, h_ref = rest
    else:
        o_ref, e_ref, s2_ref, s4_ref, s8_ref, h_ref = rest
    t = pl.program_id(1)
    x = x_ref[...]
    h = (_rms_scale(x) * g_ref[...]).astype(BF16)
    proj = jnp.dot(h, win_ref[...], preferred_element_type=F32)
    u = proj[:, :POOL_WIDTH]

    @pl.when(t == 0)
    def _():
        e_ref[0:POOL_PAD, :] = jnp.zeros((POOL_PAD, POOL_WIDTH), F32)

    @pl.when(t > 0)
    def _():
        e_ref[POOL_PAD - 16:POOL_PAD, :] = e_ref[tm + POOL_PAD - 16:tm + POOL_PAD, :]

    e_ref[POOL_PAD:, :] = u
    n = tm + POOL_PAD
    s2_ref[8:, :] = e_ref[8:, :] + e_ref[7:n - 1, :]
    s4_ref[16:, 128:] = s2_ref[16:, 128:] + s2_ref[14:n - 2, 128:]
    s8_ref[24:, 384:] = s4_ref[24:, 384:] + s4_ref[20:n - 4, 384:]
    s16 = s8_ref[32:, 512:] + s8_ref[24:n - 8, 512:]

    pos1 = t * tm + lax.broadcasted_iota(jnp.int32, (tm, 1), 0) + 1
    inv = [1.0 / jnp.minimum(pos1, w).astype(F32) for w in POOL_WINDOWS]
    lane = lax.broadcasted_iota(jnp.int32, (tm, LANES), 1)
    low = lane < (POOL_GROUP - LANES)
    grouped = jnp.concatenate([
        s2_ref[POOL_PAD:, 0:128] * inv[0],
        jnp.where(low, s2_ref[POOL_PAD:, 128:256] * inv[0], s4_ref[POOL_PAD:, 128:256] * inv[1]),
        s4_ref[POOL_PAD:, 256:384] * inv[1],
        s8_ref[POOL_PAD:, 384:512] * inv[2],
        jnp.where(low, s8_ref[POOL_PAD:, 512:640] * inv[2], s16[:, 0:128] * inv[3]),
        s16[:, 128:256] * inv[3],
    ], axis=1) - u

    qm = (proj[:, POOL_WIDTH:] * QK_SCALE_LOG2).astype(BF16)
    mem_out = _memory_attention(qm, k2_ref, v2_ref)
    o_ref[...] = (x
                  + jnp.dot(grouped.astype(BF16), wout_ref[0:POOL_WIDTH, :], preferred_element_type=F32)
                  + jnp.dot(mem_out.astype(BF16), wout_ref[POOL_WIDTH:, :], preferred_element_type=F32))

    @pl.when(t >= 0)
    def _():
        _ffn_in_place(o_ref, h_ref, gf_ref, wgu_ref, wd_ref)

    if with_qkv:
        @pl.when(t >= 0)
        def _():
            _qkv_from(o_ref[...], gkv_ref, gq_ref, wkv_ref, wq_ref, kv_ref, q_ref)


def _pool_layer(x, g, w_in_folded, k2, v2, layer, w_out, g_ffn, w_gu, w_down, tm, qkv=None):
    b, s, d = x.shape
    d_ff = w_down.shape[0]
    tile = lambda w: pl.BlockSpec((None, tm, w), lambda i, t: (i, t, 0))
    out_shape, out_specs = jax.ShapeDtypeStruct((b, s, d), F32), tile(d)
    extra_in, extra_args = [], []
    if qkv is not None:
        g_kv, w_kv, g_q, w_q = qkv
        extra_in = [_const_spec((1, d)), _const_spec((1, d)),
                    _const_spec((d, 2 * SB_WIDTH)), _const_spec((d, MIX_WIDTH))]
        extra_args = [g_kv.reshape(1, d), g_q.reshape(1, d), w_kv, w_q]
        out_shape = (out_shape, jax.ShapeDtypeStruct((b, s, 2 * SB_WIDTH), BF16),
                     jax.ShapeDtypeStruct((b, s, MIX_WIDTH), BF16))
        out_specs = (out_specs, tile(2 * SB_WIDTH), tile(MIX_WIDTH))
    return pl.pallas_call(
        functools.partial(_pool_layer_kernel, tm=tm, with_qkv=qkv is not None),
        out_shape=out_shape,
        grid=(b, s // tm),
        in_specs=[
            tile(d),
            _const_spec((1, d)),
            _const_spec((d, MIX_WIDTH)),
            _memory_kv_spec(k2, layer), _memory_kv_spec(v2, layer),
            _const_spec((MIX_WIDTH, d)),
            _const_spec((1, d)),
            _const_spec((d, 2 * d_ff)),
            _const_spec((d_ff, d)),
        ] + extra_in,
        out_specs=out_specs,
        scratch_shapes=[pltpu.VMEM((tm + POOL_PAD, POOL_WIDTH), F32)] * 4 + [pltpu.VMEM((tm, d), BF16)],
        compiler_params=pltpu.CompilerParams(
            dimension_semantics=("arbitrary", "arbitrary"), vmem_limit_bytes=VMEM_LIMIT),
        name="pool_layer_qkv" if qkv is not None else "pool_layer",
    )(x, g.reshape(1, d), w_in_folded, k2, v2, w_out, g_ffn.reshape(1, d), w_gu, w_down, *extra_args)


def _sb_kernel(q_ref, k_ref, v_ref, tri_ref, *rest, tb, n_cast):
    cast_in, (o_ref, *cast_out), scratch = rest[:n_cast], rest[n_cast:2 * n_cast + 1], rest[2 * n_cast + 1:]
    qpad_ref, acc_ref, c0_ref, c1_ref = scratch
    _cast_riders(cast_in, cast_out)

    s = q_ref.shape[0]
    nblk = s // tb
    hb = tb // 2
    pad = tb + hb
    tri = tri_ref[...]
    rows = lax.broadcasted_iota(jnp.int32, (tb, 2 * tb), 0)
    cols = lax.broadcasted_iota(jnp.int32, (tb, 2 * tb), 1)
    before = jnp.where(cols < tb, cols, cols - tb) < rows

    def softplus2(z):
        sp = jnp.maximum(z, jnp.log(1.0 + jnp.exp2(jnp.minimum(z, EXP2_MAX))) * LOG2_E)
        return sp, z - sp

    def carry_lanes(c0, c1):
        n = c0.shape[0]
        return jnp.concatenate([jnp.broadcast_to(c0, (n, tb)), jnp.broadcast_to(c1, (n, tb))], axis=1)

    def row_sums(sp):
        return jnp.sum(sp[:, :tb], axis=1, keepdims=True), jnp.sum(sp[:, tb:], axis=1, keepdims=True)

    def key_block(j):
        start = pl.multiple_of(j * tb, tb)
        return (_split_heads(k_ref[pl.ds(start, tb), :], tb),
                _split_heads(v_ref[pl.ds(start, tb), :], tb))

    def unfinished(c0, c1):
        return (jnp.min(jnp.minimum(c0, c1)) < SKIP_LOG2).astype(jnp.int32)

    qpad_ref[0:s, :] = q_ref[...]
    qpad_ref[s:, :] = jnp.zeros((pad, LANES), q_ref.dtype)

    def sweep(i, carry):
        (a_c0, a_c1, a_acc), (b_c0, b_c1, b_acc), worst = carry
        j = nblk - 1 - i
        row0 = pl.multiple_of(j * tb, tb)
        k2, v2 = key_block(j)
        q = qpad_ref[pl.ds(row0, 2 * tb + hb), :]
        z = lax.dot_general(q, k2, NT_DIMS, preferred_element_type=F32)
        sp0, lb0 = softplus2(jnp.where(before, z[0:tb], -MASKED))
        sp1, lb1 = softplus2(z[tb:2 * tb])
        sp2, lb2 = softplus2(z[2 * tb:])
        later = jnp.dot(jnp.concatenate([sp0, sp1, sp2], axis=0).astype(BF16), tri,
                        preferred_element_type=F32)
        w0 = jnp.exp2(lb0 + later[0:tb])
        w1 = jnp.exp2(lb1 + later[tb:2 * tb] - carry_lanes(a_c0, a_c1))
        w2 = jnp.exp2(lb2 + later[2 * tb:] - carry_lanes(b_c0, b_c1))
        pv = jnp.dot(jnp.concatenate([w0, w1, w2], axis=0).astype(BF16), v2, preferred_element_type=F32)
        s00, s01 = row_sums(sp0)
        s10, s11 = row_sums(sp1)
        s20, s21 = row_sums(sp2)
        a_c0, a_c1, a_acc = a_c0 + s10, a_c1 + s11, a_acc + pv[tb:2 * tb]
        b_c0, b_c1, b_acc = b_c0 + s20, b_c1 + s21, b_acc + pv[2 * tb:]
        out0 = pl.multiple_of(row0 + pad, hb)
        c0 = jnp.concatenate([a_c0[hb:], b_c0], axis=0)
        c1 = jnp.concatenate([a_c1[hb:], b_c1], axis=0)
        acc_ref[pl.ds(out0, tb), :] = jnp.concatenate([a_acc[hb:], b_acc], axis=0)
        c0_ref[pl.ds(out0, tb), :] = c0
        c1_ref[pl.ds(out0, tb), :] = c1
        worst = jnp.where(j >= 1, jnp.minimum(worst, jnp.minimum(c0, c1)), worst)
        return (s00, s01, pv[0:tb]), (a_c0[:hb], a_c1[:hb], a_acc[:hb]), worst

    def padding(n):
        return jnp.full((n, 1), MASKED, F32), jnp.full((n, 1), MASKED, F32), jnp.zeros((n, LANES), F32)

    init = (padding(tb), padding(hb), jnp.full((tb, 1), SKIP_LOG2, F32))
    a, b, worst = lax.fori_loop(0, nblk, sweep, init, unroll=SB_UNROLL if nblk % SB_UNROLL == 0 else 1)
    acc_ref[0:tb, :] = a[2]
    acc_ref[tb:pad, :] = b[2]
    o_ref[...] = acc_ref[0:s, :].astype(o_ref.dtype)

    @pl.when(unfinished(worst, worst) > 0)
    def _():
        def finish(hh, _):
            row0 = pl.multiple_of(hh * hb, hb)
            q = q_ref[pl.ds(row0, hb), :]

            def one_block(carry):
                j, _, c0, c1, acc = carry
                k2, v2 = key_block(j)
                z = lax.dot_general(q, k2, NT_DIMS, preferred_element_type=F32)
                sp, log_beta = softplus2(z)
                later = jnp.dot(sp.astype(BF16), tri, preferred_element_type=F32)
                w = jnp.exp2(log_beta + later - carry_lanes(c0, c1))
                acc = acc + jnp.dot(w.astype(BF16), v2, preferred_element_type=F32)
                s0, s1 = row_sums(sp)
                return j - 1, unfinished(c0 + s0, c1 + s1), c0 + s0, c1 + s1, acc

            first_unmet = hh // 2 - 3 + hh % 2
            c0, c1 = c0_ref[pl.ds(row0, hb), :], c1_ref[pl.ds(row0, hb), :]
            carry = (first_unmet, unfinished(c0, c1), c0, c1, acc_ref[pl.ds(row0, hb), :])
            carry = lax.while_loop(lambda c: (c[0] >= 0) & (c[1] > 0), one_block, carry)
            o_ref[pl.ds(row0, hb), :] = carry[4].astype(o_ref.dtype)
            return 0

        lax.fori_loop(3, 2 * nblk, finish, 0)


def _stick_breaking(q, kv, tb, to_cast):
    b, s, _ = q.shape
    pairs = SB_WIDTH // LANES
    cast_specs = _cast_rider_specs(to_cast, b * pairs, lambda i, p: i * pairs + p)
    pad = tb + tb // 2
    r = lax.broadcasted_iota(jnp.int32, (2 * tb, 2 * tb), 0)
    c = lax.broadcasted_iota(jnp.int32, (2 * tb, 2 * tb), 1)
    tri = -((r > c) & ((r < tb) == (c < tb))).astype(BF16)
    seq = lambda col: pl.BlockSpec((None, s, LANES), lambda i, p: (i, 0, col + p))
    return pl.pallas_call(
        functools.partial(_sb_kernel, tb=tb, n_cast=len(to_cast)),
        out_shape=(jax.ShapeDtypeStruct((b, s, SB_WIDTH), BF16),
                   *[jax.ShapeDtypeStruct(w.shape, BF16) for w in to_cast]),
        grid=(b, pairs),
        in_specs=[seq(0), seq(0), seq(pairs), _const_spec((2 * tb, 2 * tb)), *cast_specs],
        out_specs=(seq(0), *cast_specs),
        scratch_shapes=[pltpu.VMEM((s + pad, LANES), BF16), pltpu.VMEM((s + pad, LANES), F32),
                        pltpu.VMEM((s + pad, 1), F32), pltpu.VMEM((s + pad, 1), F32)],
        compiler_params=pltpu.CompilerParams(
            dimension_semantics=("arbitrary", "arbitrary"), vmem_limit_bytes=VMEM_LIMIT),
        name="stick_breaking",
    )(q, kv, kv, tri, *to_cast)


def _sb_layer_kernel(x_ref, sb_ref, qm_ref, k2_ref, v2_ref, wout_ref, gf_ref, wgu_ref, wd_ref, gn_ref, o_ref,
                     h_ref):
    mem_out = _memory_attention(qm_ref[...], k2_ref, v2_ref)
    o_ref[...] = (x_ref[...]
                  + jnp.dot(sb_ref[...], wout_ref[0:SB_WIDTH, :], preferred_element_type=F32)
                  + jnp.dot(mem_out.astype(BF16), wout_ref[SB_WIDTH:, :], preferred_element_type=F32))

    @pl.when(pl.program_id(1) >= 0)
    def _():
        _ffn_in_place(o_ref, h_ref, gf_ref, wgu_ref, wd_ref)
        o_ref[...] = _rms_scale(o_ref[...]) * gn_ref[...]


def _sb_layer(x, sb_out, q, k2, v2, layer, w_out, g_ffn, w_gu, w_down, g_final, tm):
    b, s, d = x.shape
    d_ff = w_down.shape[0]
    tile = lambda w, col=0: pl.BlockSpec((None, tm, w), lambda i, t: (i, t, col))
    return pl.pallas_call(
        _sb_layer_kernel,
        out_shape=jax.ShapeDtypeStruct((b, s, d), F32),
        grid=(b, s // tm),
        in_specs=[
            tile(d),
            tile(SB_WIDTH),
            tile(MEM_WIDTH, SB_WIDTH // MEM_WIDTH),
            _memory_kv_spec(k2, layer), _memory_kv_spec(v2, layer),
            _const_spec((MIX_WIDTH, d)),
            _const_spec((1, d)),
            _const_spec((d, 2 * d_ff)),
            _const_spec((d_ff, d)),
            _const_spec((1, d)),
        ],
        out_specs=tile(d),
        scratch_shapes=[pltpu.VMEM((tm, d), BF16)],
        compiler_params=pltpu.CompilerParams(
            dimension_semantics=("arbitrary", "arbitrary"), vmem_limit_bytes=VMEM_LIMIT),
        name="sb_layer",
    )(x, sb_out, q, k2, v2, w_out, g_ffn.reshape(1, d), w_gu, w_down, g_final.reshape(1, d))


def _block_diag(w_group):
    g, c, _ = w_group.shape
    eye = jnp.eye(g, dtype=w_group.dtype)
    return jnp.einsum("gcd,gh->gchd", w_group, eye).reshape(g * c, g * c)


def kernel(x, mem, mem_norm, a_norm_mix, a_w_in, a_w_group, a_scale, a_w_mem_kv, a_w_out, a_norm_ffn,
           a_w_gu, a_w_down, kv_norm, w_kv, b_norm_mix, b_w_q, b_w_mem_kv, b_w_out, b_norm_ffn, b_w_gu,
           b_w_down, final_norm):
    b, s, d = x.shape
    na, nb = a_w_in.shape[0], b_w_q.shape[0]
    assert na >= 1 and nb == 1, "pooling layers followed by exactly one stick-breaking layer"
    tm, tm_sb = min(ROW_TILE, s), min(SB_LAYER_ROW_TILE, s)
    assert s % tm == 0 and s % tm_sb == 0 and s % SB_TILE == 0, "sequence length must be a multiple of the row tiles"

    k2, v2 = _memory_kv(mem, mem_norm, jnp.concatenate([a_w_mem_kv, b_w_mem_kv], axis=0).astype(BF16))

    for i in range(na - 1):
        w_in, w_out, w_gu, w_down = _fold_pool_weights(a_w_in[i], _block_diag(a_w_group[i]), a_scale[i],
                                                       (a_w_out[i], a_w_gu[i], a_w_down[i]))
        x = _pool_layer(x, a_norm_mix[i], w_in, k2, v2, i, w_out, a_norm_ffn[i], w_gu, w_down, tm)
    i = na - 1
    w_in, w_out, w_gu, w_down, w_kv16, w_q16 = _fold_pool_weights(
        a_w_in[i], _block_diag(a_w_group[i]), a_scale[i], (a_w_out[i], a_w_gu[i], a_w_down[i], w_kv, b_w_q[0]))
    x, kv, q = _pool_layer(x, a_norm_mix[i], w_in, k2, v2, i, w_out, a_norm_ffn[i], w_gu, w_down, tm,
                           qkv=(kv_norm, w_kv16, b_norm_mix[0], w_q16))

    sb_out, w_out, w_gu, w_down = _stick_breaking(q, kv, SB_TILE, (b_w_out[0], b_w_gu[0], b_w_down[0]))
    return _sb_layer(x, sb_out, q, k2, v2, na, w_out, b_norm_ffn[0], w_gu, w_down, final_norm, tm_sb)
```

```python
import functools

import jax
import jax.numpy as jnp
from jax import lax
from jax.experimental import pallas as pl
from jax.experimental.pallas import tpu as pltpu

HEAD_DIM = 64
N_SB_HEADS = 12
SB_WIDTH = N_SB_HEADS * HEAD_DIM
N_MEM_HEADS = 4
MEM_WIDTH = N_MEM_HEADS * HEAD_DIM
MIX_WIDTH = SB_WIDTH + MEM_WIDTH
POOL_WINDOWS = (2, 4, 8, 16)
POOL_WIDTH = SB_WIDTH
POOL_GROUP = POOL_WIDTH // len(POOL_WINDOWS)
EPS = 1e-6
LOG2_E = 1.4426950408889634
QK_SCALE_LOG2 = HEAD_DIM ** -0.5 * LOG2_E

LANES = 128
ROW_TILE = 512
SB_LAYER_ROW_TILE = 1024
BF16_ROWS = 16
FOLD_ROWS = 128
POOL_PAD = 32
FF_CHUNK = 256
SB_TILE = 128
SB_UNROLL = 32
SKIP_LOG2 = 151.0
EXP2_MAX = 126.0
MASKED = 1e4
VMEM_LIMIT = 56 * 1024 * 1024

F32 = jnp.float32
BF16 = jnp.bfloat16
NT_DIMS = (((1,), (1,)), ((), ()))


def _const_spec(shape):
    nd = len(shape)
    return pl.BlockSpec(shape, lambda *_: (0,) * nd, pipeline_mode=pl.Buffered(1))


def _inv_rms(x):
    return lax.rsqrt(jnp.mean(x * x, axis=-1, keepdims=True) + EPS)


def _rms_scale(x):
    return x * _inv_rms(x)


def _split_heads(t, rows):
    lane = lax.broadcasted_iota(jnp.int32, (rows, LANES), 1)
    first = lane < HEAD_DIM
    zero = jnp.zeros_like(t)
    return jnp.concatenate([jnp.where(first, t, zero), jnp.where(first, zero, t)], axis=0)


def _row_blocks(rows, steps):
    for rb in range(BF16_ROWS, rows + 1, BF16_ROWS):
        if rows % rb == 0 and rows // rb <= steps:
            return rb
    raise ValueError(f"cannot split {rows} rows over {steps} steps")


def _cast_rider_specs(to_cast, steps, step_of):
    specs = []
    for w in to_cast:
        rb = _row_blocks(w.shape[0], steps)
        last = w.shape[0] // rb - 1
        specs.append(pl.BlockSpec((rb, w.shape[1]), lambda *g, last=last: (jnp.minimum(step_of(*g), last), 0)))
    return specs


def _cast_riders(cast_in, cast_out):
    for src, dst in zip(cast_in, cast_out):
        dst[...] = src[...].astype(dst.dtype)


def _memkv_kernel(mem_ref, g_ref, w_ref, k2_ref, v2_ref, *, m):
    h = (_rms_scale(mem_ref[...]) * g_ref[...]).astype(BF16)
    kv = jnp.dot(h, w_ref[...], preferred_element_type=F32)
    for i in range(k2_ref.shape[0]):
        for p in range(MEM_WIDTH // LANES):
            k = kv[i * m:(i + 1) * m, p * LANES:(p + 1) * LANES]
            v = kv[i * m:(i + 1) * m, MEM_WIDTH + p * LANES:MEM_WIDTH + (p + 1) * LANES]
            k2_ref[i, p] = _split_heads(k, m).astype(BF16)
            v2_ref[i, p] = _split_heads(v, m).astype(BF16)


def _memory_kv(mem, mem_norm, w_mem_kv):
    b, m, d = mem.shape
    n_layers = w_mem_kv.shape[0]
    pairs = MEM_WIDTH // LANES
    out = jax.ShapeDtypeStruct((n_layers, b, pairs, 2 * m, LANES), BF16)
    out_spec = pl.BlockSpec((None, b, pairs, 2 * m, LANES), lambda l: (l, 0, 0, 0, 0))
    return pl.pallas_call(
        functools.partial(_memkv_kernel, m=m),
        out_shape=(out, out),
        grid=(n_layers,),
        in_specs=[
            _const_spec((b * m, d)),
            _const_spec((1, d)),
            pl.BlockSpec((None, d, 2 * MEM_WIDTH), lambda l: (l, 0, 0)),
        ],
        out_specs=(out_spec, out_spec),
        compiler_params=pltpu.CompilerParams(dimension_semantics=("arbitrary",), vmem_limit_bytes=VMEM_LIMIT),
        name="memory_kv",
    )(mem.reshape(b * m, d), mem_norm.reshape(1, d), w_mem_kv)


def _memory_kv_spec(k2, layer):
    return pl.BlockSpec((None, None) + k2.shape[2:], lambda i, t: (layer, i, 0, 0, 0))


def _memory_attention(qm, k2_ref, v2_ref):
    t = qm.shape[0]
    m = k2_ref.shape[1] // 2
    lane = lax.broadcasted_iota(jnp.int32, (t, LANES), 1)
    outs = []
    for p in range(MEM_WIDTH // LANES):
        logits = lax.dot_general(qm[:, p * LANES:(p + 1) * LANES], k2_ref[p], NT_DIMS,
                                 preferred_element_type=F32)
        probs, inv = [], []
        for hh in range(2):
            lg = logits[:, hh * m:(hh + 1) * m]
            pe = jnp.exp2(lg - jnp.max(lg, axis=1, keepdims=True))
            probs.append(pe)
            inv.append(1.0 / jnp.sum(pe, axis=1, keepdims=True))
        pv = jnp.dot(jnp.concatenate(probs, axis=1).astype(BF16), v2_ref[p],
                     preferred_element_type=F32)
        outs.append(pv * jnp.where(lane < HEAD_DIM, inv[0], inv[1]))
    return jnp.concatenate(outs, axis=1)


def _ffn_in_place(o_ref, raw_ref, h_ref, g_ref, wgu_ref, wd_ref):
    xg = o_ref[...] * g_ref[...]
    inv_rms = _inv_rms(o_ref[...])
    raw_ref[...] = xg.astype(BF16)
    h_ref[...] = (xg * inv_rms).astype(BF16)
    d_ff = wd_ref.shape[0]
    for c in range(d_ff // FF_CHUNK):
        lo, hi = c * FF_CHUNK, (c + 1) * FF_CHUNK
        h = raw_ref[...] if c == 0 else h_ref[...]
        gate = jnp.dot(h, wgu_ref[:, lo:hi], preferred_element_type=F32)
        up = jnp.dot(h, wgu_ref[:, d_ff + lo:d_ff + hi], preferred_element_type=F32)
        if c == 0:
            gate, up = gate * inv_rms, up * inv_rms
        act = gate * (1.0 / (1.0 + jnp.exp(-gate))) * up
        o_ref[...] += jnp.dot(act.astype(BF16), wd_ref[lo:hi, :], preferred_element_type=F32)


def _fold_kernel(win_ref, wgrp_ref, scale_ref, *rest, n_cast):
    o_ref = rest[n_cast]
    _cast_riders(rest[:n_cast], rest[n_cast + 1:])
    a = win_ref[:, :POOL_WIDTH]
    b = wgrp_ref[...]
    a_hi, b_hi = a.astype(BF16), b.astype(BF16)
    a_lo, b_lo = (a - a_hi.astype(F32)).astype(BF16), (b - b_hi.astype(F32)).astype(BF16)
    prod = (jnp.dot(a_hi, b_hi, preferred_element_type=F32) + jnp.dot(a_hi, b_lo, preferred_element_type=F32)
            + jnp.dot(a_lo, b_hi, preferred_element_type=F32))
    o_ref[:, :POOL_WIDTH] = (prod * scale_ref[...]).astype(BF16)
    o_ref[:, POOL_WIDTH:] = win_ref[:, POOL_WIDTH:].astype(BF16)


def _fold_pool_weights(w_in, w_group_full, scale, to_cast):
    d = w_in.shape[0]
    tr = min(FOLD_ROWS, d)
    rider_specs = _cast_rider_specs(to_cast, d // tr, lambda i: i)
    folded_spec = pl.BlockSpec((tr, MIX_WIDTH), lambda i: (i, 0))
    return pl.pallas_call(
        functools.partial(_fold_kernel, n_cast=len(to_cast)),
        out_shape=(jax.ShapeDtypeStruct((d, MIX_WIDTH), BF16), *[jax.ShapeDtypeStruct(w.shape, BF16) for w in to_cast]),
        grid=(d // tr,),
        in_specs=[folded_spec, _const_spec((POOL_WIDTH, POOL_WIDTH)), _const_spec((1, POOL_WIDTH)), *rider_specs],
        out_specs=(folded_spec, *rider_specs),
        compiler_params=pltpu.CompilerParams(dimension_semantics=("arbitrary",), vmem_limit_bytes=VMEM_LIMIT),
        name="fold_pool_weights",
    )(w_in, w_group_full, scale.reshape(1, POOL_WIDTH), *to_cast)


def _qkv_from(x, gkv_ref, gq_ref, wkv_ref, wq_ref, kv_ref, q_ref):
    inv_rms = _inv_rms(x)
    kv = jnp.dot((x * gkv_ref[...]).astype(BF16), wkv_ref[...], preferred_element_type=F32)
    kv_ref[...] = (kv * inv_rms).astype(BF16)
    q = jnp.dot((x * gq_ref[...]).astype(BF16), wq_ref[...], preferred_element_type=F32)
    q_ref[...] = (q * (inv_rms * QK_SCALE_LOG2)).astype(BF16)


def _pool_layer_kernel(x_ref, g_ref, win_ref, k2_ref, v2_ref, wout_ref, gf_ref, wgu_ref, wd_ref, *rest,
                       tm, with_qkv):
    if with_qkv:
        gkv_ref, gq_ref, wkv_ref, wq_ref, o_ref, kv_ref, q_ref, e_ref, s2_ref, s4_ref, s8_ref, raw_ref, h_ref = rest
    else:
        o_ref, e_ref, s2_ref, s4_ref, s8_ref, raw_ref, h_ref = rest
    t = pl.program_id(1)
    x = x_ref[...]
    proj = jnp.dot((x * g_ref[...]).astype(BF16), win_ref[...], preferred_element_type=F32) * _inv_rms(x)
    u = proj[:, :POOL_WIDTH]

    @pl.when(t == 0)
    def _():
        e_ref[0:POOL_PAD, :] = jnp.zeros((POOL_PAD, POOL_WIDTH), F32)

    @pl.when(t > 0)
    def _():
        e_ref[POOL_PAD - 16:POOL_PAD, :] = e_ref[tm + POOL_PAD - 16:tm + POOL_PAD, :]

    e_ref[POOL_PAD:, :] = u
    n = tm + POOL_PAD
    s2_ref[8:, :] = e_ref[8:, :] + e_ref[7:n - 1, :]
    s4_ref[16:, 128:] = s2_ref[16:, 128:] + s2_ref[14:n - 2, 128:]
    s8_ref[24:, 384:] = s4_ref[24:, 384:] + s4_ref[20:n - 4, 384:]
    s16 = s8_ref[32:, 512:] + s8_ref[24:n - 8, 512:]

    pos1 = t * tm + lax.broadcasted_iota(jnp.int32, (tm, 1), 0) + 1
    inv = [1.0 / jnp.minimum(pos1, w).astype(F32) for w in POOL_WINDOWS]
    lane = lax.broadcasted_iota(jnp.int32, (tm, LANES), 1)
    low = lane < (POOL_GROUP - LANES)
    grouped = jnp.concatenate([
        s2_ref[POOL_PAD:, 0:128] * inv[0],
        jnp.where(low, s2_ref[POOL_PAD:, 128:256] * inv[0], s4_ref[POOL_PAD:, 128:256] * inv[1]),
        s4_ref[POOL_PAD:, 256:384] * inv[1],
        s8_ref[POOL_PAD:, 384:512] * inv[2],
        jnp.where(low, s8_ref[POOL_PAD:, 512:640] * inv[2], s16[:, 0:128] * inv[3]),
        s16[:, 128:256] * inv[3],
    ], axis=1) - u

    qm = (proj[:, POOL_WIDTH:] * QK_SCALE_LOG2).astype(BF16)
    mem_out = _memory_attention(qm, k2_ref, v2_ref)
    o_ref[...] = (x
                  + jnp.dot(grouped.astype(BF16), wout_ref[0:POOL_WIDTH, :], preferred_element_type=F32)
                  + jnp.dot(mem_out.astype(BF16), wout_ref[POOL_WIDTH:, :], preferred_element_type=F32))

    @pl.when(t >= 0)
    def _():
        _ffn_in_place(o_ref, raw_ref, h_ref, gf_ref, wgu_ref, wd_ref)

    if with_qkv:
        @pl.when(t >= 0)
        def _():
            _qkv_from(o_ref[...], gkv_ref, gq_ref, wkv_ref, wq_ref, kv_ref, q_ref)


def _pool_layer(x, g, w_in_folded, k2, v2, layer, w_out, g_ffn, w_gu, w_down, tm, qkv=None):
    b, s, d = x.shape
    d_ff = w_down.shape[0]
    tile = lambda w: pl.BlockSpec((None, tm, w), lambda i, t: (i, t, 0))
    out_shape, out_specs = jax.ShapeDtypeStruct((b, s, d), F32), tile(d)
    extra_in, extra_args = [], []
    if qkv is not None:
        g_kv, w_kv, g_q, w_q = qkv
        extra_in = [_const_spec((1, d)), _const_spec((1, d)),
                    _const_spec((d, 2 * SB_WIDTH)), _const_spec((d, MIX_WIDTH))]
        extra_args = [g_kv.reshape(1, d), g_q.reshape(1, d), w_kv, w_q]
        out_shape = (out_shape, jax.ShapeDtypeStruct((b, s, 2 * SB_WIDTH), BF16),
                     jax.ShapeDtypeStruct((b, s, MIX_WIDTH), BF16))
        out_specs = (out_specs, tile(2 * SB_WIDTH), tile(MIX_WIDTH))
    return pl.pallas_call(
        functools.partial(_pool_layer_kernel, tm=tm, with_qkv=qkv is not None),
        out_shape=out_shape,
        grid=(b, s // tm),
        in_specs=[
            tile(d),
            _const_spec((1, d)),
            _const_spec((d, MIX_WIDTH)),
            _memory_kv_spec(k2, layer), _memory_kv_spec(v2, layer),
            _const_spec((MIX_WIDTH, d)),
            _const_spec((1, d)),
            _const_spec((d, 2 * d_ff)),
            _const_spec((d_ff, d)),
        ] + extra_in,
        out_specs=out_specs,
        scratch_shapes=[pltpu.VMEM((tm + POOL_PAD, POOL_WIDTH), F32)] * 4 + [pltpu.VMEM((tm, d), BF16)] * 2,
        compiler_params=pltpu.CompilerParams(
            dimension_semantics=("arbitrary", "arbitrary"), vmem_limit_bytes=VMEM_LIMIT),
        name="pool_layer_qkv" if qkv is not None else "pool_layer",
    )(x, g.reshape(1, d), w_in_folded, k2, v2, w_out, g_ffn.reshape(1, d), w_gu, w_down, *extra_args)


def _sb_kernel(q_ref, k_ref, v_ref, tri_ref, *rest, tb, n_cast):
    cast_in, (o_ref, *cast_out), scratch = rest[:n_cast], rest[n_cast:2 * n_cast + 1], rest[2 * n_cast + 1:]
    qpad_ref, acc_ref, c0_ref, c1_ref = scratch
    _cast_riders(cast_in, cast_out)

    s = q_ref.shape[0]
    nblk = s // tb
    hb = tb // 2
    pad = tb + hb
    tri = tri_ref[...]
    rows = lax.broadcasted_iota(jnp.int32, (tb, 2 * tb), 0)
    cols = lax.broadcasted_iota(jnp.int32, (tb, 2 * tb), 1)
    before = jnp.where(cols < tb, cols, cols - tb) < rows

    def softplus2(z):
        sp = jnp.maximum(z, jnp.log(1.0 + jnp.exp2(jnp.minimum(z, EXP2_MAX))) * LOG2_E)
        return sp, z - sp

    def carry_lanes(c0, c1):
        n = c0.shape[0]
        return jnp.concatenate([jnp.broadcast_to(c0, (n, tb)), jnp.broadcast_to(c1, (n, tb))], axis=1)

    def row_sums(sp):
        return jnp.sum(sp[:, :tb], axis=1, keepdims=True), jnp.sum(sp[:, tb:], axis=1, keepdims=True)

    def key_block(j):
        start = pl.multiple_of(j * tb, tb)
        return (_split_heads(k_ref[pl.ds(start, tb), :], tb),
                _split_heads(v_ref[pl.ds(start, tb), :], tb))

    def unfinished(c0, c1):
        return (jnp.min(jnp.minimum(c0, c1)) < SKIP_LOG2).astype(jnp.int32)

    qpad_ref[0:s, :] = q_ref[...]
    qpad_ref[s:, :] = jnp.zeros((pad, LANES), q_ref.dtype)

    def sweep(i, carry):
        (a_c0, a_c1, a_acc), (b_c0, b_c1, b_acc), worst = carry
        j = nblk - 1 - i
        row0 = pl.multiple_of(j * tb, tb)
        k2, v2 = key_block(j)
        q = qpad_ref[pl.ds(row0, 2 * tb + hb), :]
        z = lax.dot_general(q, k2, NT_DIMS, preferred_element_type=F32)
        sp0, lb0 = softplus2(jnp.where(before, z[0:tb], -MASKED))
        sp1, lb1 = softplus2(z[tb:2 * tb])
        sp2, lb2 = softplus2(z[2 * tb:])
        later = jnp.dot(jnp.concatenate([sp0, sp1, sp2], axis=0).astype(BF16), tri,
                        preferred_element_type=F32)
        w0 = jnp.exp2(lb0 + later[0:tb])
        w1 = jnp.exp2(lb1 + later[tb:2 * tb] - carry_lanes(a_c0, a_c1))
        w2 = jnp.exp2(lb2 + later[2 * tb:] - carry_lanes(b_c0, b_c1))
        pv = jnp.dot(jnp.concatenate([w0, w1, w2], axis=0).astype(BF16), v2, preferred_element_type=F32)
        s00, s01 = row_sums(sp0)
        s10, s11 = row_sums(sp1)
        s20, s21 = row_sums(sp2)
        a_c0, a_c1, a_acc = a_c0 + s10, a_c1 + s11, a_acc + pv[tb:2 * tb]
        b_c0, b_c1, b_acc = b_c0 + s20, b_c1 + s21, b_acc + pv[2 * tb:]
        out0 = pl.multiple_of(row0 + pad, hb)
        c0 = jnp.concatenate([a_c0[hb:], b_c0], axis=0)
        c1 = jnp.concatenate([a_c1[hb:], b_c1], axis=0)
        acc_ref[pl.ds(out0, tb), :] = jnp.concatenate([a_acc[hb:], b_acc], axis=0)
        c0_ref[pl.ds(out0, tb), :] = c0
        c1_ref[pl.ds(out0, tb), :] = c1
        worst = jnp.where(j >= 1, jnp.minimum(worst, jnp.minimum(c0, c1)), worst)
        return (s00, s01, pv[0:tb]), (a_c0[:hb], a_c1[:hb], a_acc[:hb]), worst

    def padding(n):
        return jnp.full((n, 1), MASKED, F32), jnp.full((n, 1), MASKED, F32), jnp.zeros((n, LANES), F32)

    init = (padding(tb), padding(hb), jnp.full((tb, 1), SKIP_LOG2, F32))
    a, b, worst = lax.fori_loop(0, nblk, sweep, init, unroll=SB_UNROLL if nblk % SB_UNROLL == 0 else 1)
    acc_ref[0:tb, :] = a[2]
    acc_ref[tb:pad, :] = b[2]
    o_ref[...] = acc_ref[0:s, :].astype(o_ref.dtype)

    @pl.when(unfinished(worst, worst) > 0)
    def _():
        def finish(hh, _):
            row0 = pl.multiple_of(hh * hb, hb)
            q = q_ref[pl.ds(row0, hb), :]

            def one_block(carry):
                j, _, c0, c1, acc = carry
                k2, v2 = key_block(j)
                z = lax.dot_general(q, k2, NT_DIMS, preferred_element_type=F32)
                sp, log_beta = softplus2(z)
                later = jnp.dot(sp.astype(BF16), tri, preferred_element_type=F32)
                w = jnp.exp2(log_beta + later - carry_lanes(c0, c1))
                acc = acc + jnp.dot(w.astype(BF16), v2, preferred_element_type=F32)
                s0, s1 = row_sums(sp)
                return j - 1, unfinished(c0 + s0, c1 + s1), c0 + s0, c1 + s1, acc

            first_unmet = hh // 2 - 3 + hh % 2
            c0, c1 = c0_ref[pl.ds(row0, hb), :], c1_ref[pl.ds(row0, hb), :]
            carry = (first_unmet, unfinished(c0, c1), c0, c1, acc_ref[pl.ds(row0, hb), :])
            carry = lax.while_loop(lambda c: (c[0] >= 0) & (c[1] > 0), one_block, carry)
            o_ref[pl.ds(row0, hb), :] = carry[4].astype(o_ref.dtype)
            return 0

        lax.fori_loop(3, 2 * nblk, finish, 0)


def _stick_breaking(q, kv, tb, to_cast):
    b, s, _ = q.shape
    pairs = SB_WIDTH // LANES
    cast_specs = _cast_rider_specs(to_cast, b * pairs, lambda i, p: i * pairs + p)
    pad = tb + tb // 2
    r = lax.broadcasted_iota(jnp.int32, (2 * tb, 2 * tb), 0)
    c = lax.broadcasted_iota(jnp.int32, (2 * tb, 2 * tb), 1)
    tri = -((r > c) & ((r < tb) == (c < tb))).astype(BF16)
    seq = lambda col: pl.BlockSpec((None, s, LANES), lambda i, p: (i, 0, col + p))
    return pl.pallas_call(
        functools.partial(_sb_kernel, tb=tb, n_cast=len(to_cast)),
        out_shape=(jax.ShapeDtypeStruct((b, s, SB_WIDTH), BF16),
                   *[jax.ShapeDtypeStruct(w.shape, BF16) for w in to_cast]),
        grid=(b, pairs),
        in_specs=[seq(0), seq(0), seq(pairs), _const_spec((2 * tb, 2 * tb)), *cast_specs],
        out_specs=(seq(0), *cast_specs),
        scratch_shapes=[pltpu.VMEM((s + pad, LANES), BF16), pltpu.VMEM((s + pad, LANES), F32),
                        pltpu.VMEM((s + pad, 1), F32), pltpu.VMEM((s + pad, 1), F32)],
        compiler_params=pltpu.CompilerParams(
            dimension_semantics=("arbitrary", "arbitrary"), vmem_limit_bytes=VMEM_LIMIT),
        name="stick_breaking",
    )(q, kv, kv, tri, *to_cast)


def _sb_layer_kernel(x_ref, sb_ref, qm_ref, k2_ref, v2_ref, wout_ref, gf_ref, wgu_ref, wd_ref, gn_ref, o_ref,
                     raw_ref, h_ref):
    mem_out = _memory_attention(qm_ref[...], k2_ref, v2_ref)
    o_ref[...] = (x_ref[...]
                  + jnp.dot(sb_ref[...], wout_ref[0:SB_WIDTH, :], preferred_element_type=F32)
                  + jnp.dot(mem_out.astype(BF16), wout_ref[SB_WIDTH:, :], preferred_element_type=F32))

    @pl.when(pl.program_id(1) >= 0)
    def _():
        _ffn_in_place(o_ref, raw_ref, h_ref, gf_ref, wgu_ref, wd_ref)
        o_ref[...] = _rms_scale(o_ref[...]) * gn_ref[...]


def _sb_layer(x, sb_out, q, k2, v2, layer, w_out, g_ffn, w_gu, w_down, g_final, tm):
    b, s, d = x.shape
    d_ff = w_down.shape[0]
    tile = lambda w, col=0: pl.BlockSpec((None, tm, w), lambda i, t: (i, t, col))
    return pl.pallas_call(
        _sb_layer_kernel,
        out_shape=jax.ShapeDtypeStruct((b, s, d), F32),
        grid=(b, s // tm),
        in_specs=[
            tile(d),
            tile(SB_WIDTH),
            tile(MEM_WIDTH, SB_WIDTH // MEM_WIDTH),
            _memory_kv_spec(k2, layer), _memory_kv_spec(v2, layer),
            _const_spec((MIX_WIDTH, d)),
            _const_spec((1, d)),
            _const_spec((d, 2 * d_ff)),
            _const_spec((d_ff, d)),
            _const_spec((1, d)),
        ],
        out_specs=tile(d),
        scratch_shapes=[pltpu.VMEM((tm, d), BF16)] * 2,
        compiler_params=pltpu.CompilerParams(
            dimension_semantics=("arbitrary", "arbitrary"), vmem_limit_bytes=VMEM_LIMIT),
        name="sb_layer",
    )(x, sb_out, q, k2, v2, w_out, g_ffn.reshape(1, d), w_gu, w_down, g_final.reshape(1, d))


def _block_diag(w_group):
    g, c, _ = w_group.shape
    eye = jnp.eye(g, dtype=w_group.dtype)
    return jnp.einsum("gcd,gh->gchd", w_group, eye).reshape(g * c, g * c)


def kernel(x, mem, mem_norm, a_norm_mix, a_w_in, a_w_group, a_scale, a_w_mem_kv, a_w_out, a_norm_ffn,
           a_w_gu, a_w_down, kv_norm, w_kv, b_norm_mix, b_w_q, b_w_mem_kv, b_w_out, b_norm_ffn, b_w_gu,
           b_w_down, final_norm):
    b, s, d = x.shape
    na, nb = a_w_in.shape[0], b_w_q.shape[0]
    assert na >= 1 and nb == 1, "pooling layers followed by exactly one stick-breaking layer"
    tm, tm_sb = min(ROW_TILE, s), min(SB_LAYER_ROW_TILE, s)
    assert s % tm == 0 and s % tm_sb == 0 and s % SB_TILE == 0, "sequence length must be a multiple of the row tiles"

    k2, v2 = _memory_kv(mem, mem_norm, jnp.concatenate([a_w_mem_kv, b_w_mem_kv], axis=0).astype(BF16))

    for i in range(na - 1):
        w_in, w_out, w_gu, w_down = _fold_pool_weights(a_w_in[i], _block_diag(a_w_group[i]), a_scale[i],
                                                       (a_w_out[i], a_w_gu[i], a_w_down[i]))
        x = _pool_layer(x, a_norm_mix[i], w_in, k2, v2, i, w_out, a_norm_ffn[i], w_gu, w_down, tm)
    i = na - 1
    w_in, w_out, w_gu, w_down, w_kv16, w_q16 = _fold_pool_weights(
        a_w_in[i], _block_diag(a_w_group[i]), a_scale[i], (a_w_out[i], a_w_gu[i], a_w_down[i], w_kv, b_w_q[0]))
    x, kv, q = _pool_layer(x, a_norm_mix[i], w_in, k2, v2, i, w_out, a_norm_ffn[i], w_gu, w_down, tm,
                           qkv=(kv_norm, w_kv16, b_norm_mix[0], w_q16))

    sb_out, w_out, w_gu, w_down = _stick_breaking(q, kv, SB_TILE, (b_w_out[0], b_w_gu[0], b_w_down[0]))
    return _sb_layer(x, sb_out, q, k2, v2, na, w_out, b_norm_ffn[0], w_gu, w_down, final_norm, tm_sb)
```

```python
import functools

import jax
import jax.numpy as jnp
from jax import lax
from jax.experimental import pallas as pl
from jax.experimental.pallas import tpu as pltpu

HEAD_DIM = 64
N_SB_HEADS = 12
SB_WIDTH = N_SB_HEADS * HEAD_DIM
N_MEM_HEADS = 4
MEM_WIDTH = N_MEM_HEADS * HEAD_DIM
MIX_WIDTH = SB_WIDTH + MEM_WIDTH
POOL_WINDOWS = (2, 4, 8, 16)
POOL_WIDTH = SB_WIDTH
POOL_GROUP = POOL_WIDTH // len(POOL_WINDOWS)
EPS = 1e-6
LOG2_E = 1.4426950408889634
QK_SCALE_LOG2 = HEAD_DIM ** -0.5 * LOG2_E

LANES = 128
ROW_TILE = 512
SB_LAYER_ROW_TILE = 1024
BF16_ROWS = 16
FOLD_ROWS = 128
POOL_PAD = 32
FF_CHUNK = 256
SB_TILE = 128
SB_UNROLL = 32
SKIP_LOG2 = 151.0
EXP2_MAX = 126.0
MASKED = 1e4
VMEM_LIMIT = 56 * 1024 * 1024

F32 = jnp.float32
BF16 = jnp.bfloat16
NT_DIMS = (((1,), (1,)), ((), ()))


def _const_spec(shape):
    nd = len(shape)
    return pl.BlockSpec(shape, lambda *_: (0,) * nd, pipeline_mode=pl.Buffered(1))


def _inv_rms(x):
    return lax.rsqrt(jnp.mean(x * x, axis=-1, keepdims=True) + EPS)


def _rms_scale(x):
    return x * _inv_rms(x)


def _split_heads(t, rows):
    lane = lax.broadcasted_iota(jnp.int32, (rows, LANES), 1)
    first = lane < HEAD_DIM
    zero = jnp.zeros_like(t)
    return jnp.concatenate([jnp.where(first, t, zero), jnp.where(first, zero, t)], axis=0)


def _row_blocks(rows, steps):
    for rb in range(BF16_ROWS, rows + 1, BF16_ROWS):
        if rows % rb == 0 and rows // rb <= steps:
            return rb
    raise ValueError(f"cannot split {rows} rows over {steps} steps")


def _cast_rider_specs(to_cast, steps, step_of):
    specs = []
    for w in to_cast:
        rb = _row_blocks(w.shape[0], steps)
        last = w.shape[0] // rb - 1
        specs.append(pl.BlockSpec((rb, w.shape[1]), lambda *g, last=last: (jnp.minimum(step_of(*g), last), 0)))
    return specs


def _cast_riders(cast_in, cast_out):
    for src, dst in zip(cast_in, cast_out):
        dst[...] = src[...].astype(dst.dtype)


def _memory_kv_step(mem_ref, g_ref, w_ref, k2_ref, v2_ref):
    m = mem_ref.shape[0]
    h = (_rms_scale(mem_ref[...]) * g_ref[...]).astype(BF16)
    for layer in range(w_ref.shape[0]):
        kv = jnp.dot(h, w_ref[layer], preferred_element_type=F32)
        for p in range(MEM_WIDTH // LANES):
            k2_ref[layer, p] = _split_heads(kv[:, p * LANES:(p + 1) * LANES], m).astype(BF16)
            v2_ref[layer, p] = _split_heads(kv[:, MEM_WIDTH + p * LANES:MEM_WIDTH + (p + 1) * LANES], m).astype(BF16)


def _memory_kv_spec(k2, layer):
    return pl.BlockSpec((None, None) + k2.shape[2:], lambda i, t: (i, layer, 0, 0, 0))


def _memory_attention(qm, k2_ref, v2_ref):
    t = qm.shape[0]
    m = k2_ref.shape[1] // 2
    lane = lax.broadcasted_iota(jnp.int32, (t, LANES), 1)
    outs = []
    for p in range(MEM_WIDTH // LANES):
        logits = lax.dot_general(qm[:, p * LANES:(p + 1) * LANES], k2_ref[p], NT_DIMS,
                                 preferred_element_type=F32)
        probs, inv = [], []
        for hh in range(2):
            lg = logits[:, hh * m:(hh + 1) * m]
            pe = jnp.exp2(lg - jnp.max(lg, axis=1, keepdims=True))
            probs.append(pe)
            inv.append(1.0 / jnp.sum(pe, axis=1, keepdims=True))
        pv = jnp.dot(jnp.concatenate(probs, axis=1).astype(BF16), v2_ref[p],
                     preferred_element_type=F32)
        outs.append(pv * jnp.where(lane < HEAD_DIM, inv[0], inv[1]))
    return jnp.concatenate(outs, axis=1)


def _ffn_in_place(o_ref, raw_ref, h_ref, g_ref, wgu_ref, wd_ref):
    xg = o_ref[...] * g_ref[...]
    inv_rms = _inv_rms(o_ref[...])
    raw_ref[...] = xg.astype(BF16)
    h_ref[...] = (xg * inv_rms).astype(BF16)
    d_ff = wd_ref.shape[0]
    for c in range(d_ff // FF_CHUNK):
        lo, hi = c * FF_CHUNK, (c + 1) * FF_CHUNK
        h = raw_ref[...] if c == 0 else h_ref[...]
        gate = jnp.dot(h, wgu_ref[:, lo:hi], preferred_element_type=F32)
        up = jnp.dot(h, wgu_ref[:, d_ff + lo:d_ff + hi], preferred_element_type=F32)
        if c == 0:
            gate, up = gate * inv_rms, up * inv_rms
        act = gate * (1.0 / (1.0 + jnp.exp(-gate))) * up
        o_ref[...] += jnp.dot(act.astype(BF16), wd_ref[lo:hi, :], preferred_element_type=F32)


def _fold_kernel(win_ref, wgrp_ref, scale_ref, mem_ref, gmem_ref, wmem_ref, *rest, n_cast):
    o_ref, k2_ref, v2_ref = rest[n_cast:n_cast + 3]
    _cast_riders(rest[:n_cast], rest[n_cast + 3:])
    _memory_kv_step(mem_ref, gmem_ref, wmem_ref, k2_ref, v2_ref)
    a = win_ref[:, :POOL_WIDTH]
    b = wgrp_ref[...]
    a_hi, b_hi = a.astype(BF16), b.astype(BF16)
    a_lo, b_lo = (a - a_hi.astype(F32)).astype(BF16), (b - b_hi.astype(F32)).astype(BF16)
    prod = (jnp.dot(a_hi, b_hi, preferred_element_type=F32) + jnp.dot(a_hi, b_lo, preferred_element_type=F32)
            + jnp.dot(a_lo, b_hi, preferred_element_type=F32))
    o_ref[:, :POOL_WIDTH] = (prod * scale_ref[...]).astype(BF16)
    o_ref[:, POOL_WIDTH:] = win_ref[:, POOL_WIDTH:].astype(BF16)


def _prepare_weights(w_in, w_group_full, scale, mem, mem_norm, w_mem_kv, to_cast):
    d = w_in.shape[0]
    b, m, _ = mem.shape
    n_layers = w_mem_kv.shape[0]
    pairs = MEM_WIDTH // LANES
    tr = min(FOLD_ROWS, d)
    steps = d // tr
    assert steps >= b, "one memory batch element per grid step"
    rider_specs = _cast_rider_specs(to_cast, steps, lambda i: i)
    folded_spec = pl.BlockSpec((tr, MIX_WIDTH), lambda i: (i, 0))
    batch = lambda i: jnp.minimum(i, b - 1)
    kv_shape = jax.ShapeDtypeStruct((b, n_layers, pairs, 2 * m, LANES), BF16)
    kv_spec = pl.BlockSpec((None, n_layers, pairs, 2 * m, LANES), lambda i: (batch(i), 0, 0, 0, 0))
    return pl.pallas_call(
        functools.partial(_fold_kernel, n_cast=len(to_cast)),
        out_shape=(jax.ShapeDtypeStruct((d, MIX_WIDTH), BF16), kv_shape, kv_shape,
                   *[jax.ShapeDtypeStruct(w.shape, BF16) for w in to_cast]),
        grid=(steps,),
        in_specs=[folded_spec, _const_spec((POOL_WIDTH, POOL_WIDTH)), _const_spec((1, POOL_WIDTH)),
                  pl.BlockSpec((None, m, d), lambda i: (batch(i), 0, 0)), _const_spec((1, d)),
                  _const_spec((n_layers, d, 2 * MEM_WIDTH)), *rider_specs],
        out_specs=(folded_spec, kv_spec, kv_spec, *rider_specs),
        compiler_params=pltpu.CompilerParams(dimension_semantics=("arbitrary",), vmem_limit_bytes=VMEM_LIMIT),
        name="prepare_weights",
    )(w_in, w_group_full, scale.reshape(1, POOL_WIDTH), mem, mem_norm.reshape(1, d), w_mem_kv, *to_cast)


def _qkv_from(x, gkv_ref, gq_ref, wkv_ref, wq_ref, kv_ref, q_ref):
    inv_rms = _inv_rms(x)
    kv = jnp.dot((x * gkv_ref[...]).astype(BF16), wkv_ref[...], preferred_element_type=F32)
    kv_ref[...] = (kv * inv_rms).astype(BF16)
    q = jnp.dot((x * gq_ref[...]).astype(BF16), wq_ref[...], preferred_element_type=F32)
    q_ref[...] = (q * (inv_rms * QK_SCALE_LOG2)).astype(BF16)


def _pool_layer_kernel(x_ref, g_ref, win_ref, k2_ref, v2_ref, wout_ref, gf_ref, wgu_ref, wd_ref, *rest,
                       tm, with_qkv):
    if with_qkv:
        gkv_ref, gq_ref, wkv_ref, wq_ref, o_ref, kv_ref, q_ref, e_ref, s2_ref, s4_ref, s8_ref, raw_ref, h_ref = rest
    else:
        o_ref, e_ref, s2_ref, s4_ref, s8_ref, raw_ref, h_ref = rest
    t = pl.program_id(1)
    x = x_ref[...]
    proj = jnp.dot((x * g_ref[...]).astype(BF16), win_ref[...], preferred_element_type=F32) * _inv_rms(x)
    u = proj[:, :POOL_WIDTH]

    @pl.when(t == 0)
    def _():
        e_ref[0:POOL_PAD, :] = jnp.zeros((POOL_PAD, POOL_WIDTH), F32)

    @pl.when(t > 0)
    def _():
        e_ref[POOL_PAD - 16:POOL_PAD, :] = e_ref[tm + POOL_PAD - 16:tm + POOL_PAD, :]

    e_ref[POOL_PAD:, :] = u
    n = tm + POOL_PAD
    s2_ref[8:, :] = e_ref[8:, :] + e_ref[7:n - 1, :]
    s4_ref[16:, 128:] = s2_ref[16:, 128:] + s2_ref[14:n - 2, 128:]
    s8_ref[24:, 384:] = s4_ref[24:, 384:] + s4_ref[20:n - 4, 384:]
    s16 = s8_ref[32:, 512:] + s8_ref[24:n - 8, 512:]

    pos1 = t * tm + lax.broadcasted_iota(jnp.int32, (tm, 1), 0) + 1
    inv = [1.0 / jnp.minimum(pos1, w).astype(F32) for w in POOL_WINDOWS]
    lane = lax.broadcasted_iota(jnp.int32, (tm, LANES), 1)
    low = lane < (POOL_GROUP - LANES)
    grouped = jnp.concatenate([
        s2_ref[POOL_PAD:, 0:128] * inv[0],
        jnp.where(low, s2_ref[POOL_PAD:, 128:256] * inv[0], s4_ref[POOL_PAD:, 128:256] * inv[1]),
        s4_ref[POOL_PAD:, 256:384] * inv[1],
        s8_ref[POOL_PAD:, 384:512] * inv[2],
        jnp.where(low, s8_ref[POOL_PAD:, 512:640] * inv[2], s16[:, 0:128] * inv[3]),
        s16[:, 128:256] * inv[3],
    ], axis=1) - u

    qm = (proj[:, POOL_WIDTH:] * QK_SCALE_LOG2).astype(BF16)
    mem_out = _memory_attention(qm, k2_ref, v2_ref)
    o_ref[...] = (x
                  + jnp.dot(grouped.astype(BF16), wout_ref[0:POOL_WIDTH, :], preferred_element_type=F32)
                  + jnp.dot(mem_out.astype(BF16), wout_ref[POOL_WIDTH:, :], preferred_element_type=F32))

    @pl.when(t >= 0)
    def _():
        _ffn_in_place(o_ref, raw_ref, h_ref, gf_ref, wgu_ref, wd_ref)

    if with_qkv:
        @pl.when(t >= 0)
        def _():
            _qkv_from(o_ref[...], gkv_ref, gq_ref, wkv_ref, wq_ref, kv_ref, q_ref)


def _pool_layer(x, g, w_in_folded, k2, v2, layer, w_out, g_ffn, w_gu, w_down, tm, qkv=None):
    b, s, d = x.shape
    d_ff = w_down.shape[0]
    tile = lambda w: pl.BlockSpec((None, tm, w), lambda i, t: (i, t, 0))
    out_shape, out_specs = jax.ShapeDtypeStruct((b, s, d), F32), tile(d)
    extra_in, extra_args = [], []
    if qkv is not None:
        g_kv, w_kv, g_q, w_q = qkv
        extra_in = [_const_spec((1, d)), _const_spec((1, d)),
                    _const_spec((d, 2 * SB_WIDTH)), _const_spec((d, MIX_WIDTH))]
        extra_args = [g_kv.reshape(1, d), g_q.reshape(1, d), w_kv, w_q]
        out_shape = (out_shape, jax.ShapeDtypeStruct((b, s, 2 * SB_WIDTH), BF16),
                     jax.ShapeDtypeStruct((b, s, MIX_WIDTH), BF16))
        out_specs = (out_specs, tile(2 * SB_WIDTH), tile(MIX_WIDTH))
    return pl.pallas_call(
        functools.partial(_pool_layer_kernel, tm=tm, with_qkv=qkv is not None),
        out_shape=out_shape,
        grid=(b, s // tm),
        in_specs=[
            tile(d),
            _const_spec((1, d)),
            _const_spec((d, MIX_WIDTH)),
            _memory_kv_spec(k2, layer), _memory_kv_spec(v2, layer),
            _const_spec((MIX_WIDTH, d)),
            _const_spec((1, d)),
            _const_spec((d, 2 * d_ff)),
            _const_spec((d_ff, d)),
        ] + extra_in,
        out_specs=out_specs,
        scratch_shapes=[pltpu.VMEM((tm + POOL_PAD, POOL_WIDTH), F32)] * 4 + [pltpu.VMEM((tm, d), BF16)] * 2,
        compiler_params=pltpu.CompilerParams(
            dimension_semantics=("arbitrary", "arbitrary"), vmem_limit_bytes=VMEM_LIMIT),
        name="pool_layer_qkv" if qkv is not None else "pool_layer",
    )(x, g.reshape(1, d), w_in_folded, k2, v2, w_out, g_ffn.reshape(1, d), w_gu, w_down, *extra_args)


def _sb_kernel(q_ref, k_ref, v_ref, tri_ref, *rest, tb, n_cast):
    cast_in, (o_ref, *cast_out), scratch = rest[:n_cast], rest[n_cast:2 * n_cast + 1], rest[2 * n_cast + 1:]
    qpad_ref, acc_ref, c0_ref, c1_ref = scratch
    _cast_riders(cast_in, cast_out)

    s = q_ref.shape[0]
    nblk = s // tb
    hb = tb // 2
    pad = tb + hb
    tri = tri_ref[...]
    rows = lax.broadcasted_iota(jnp.int32, (tb, 2 * tb), 0)
    cols = lax.broadcasted_iota(jnp.int32, (tb, 2 * tb), 1)
    before = jnp.where(cols < tb, cols, cols - tb) < rows

    def softplus2(z):
        sp = jnp.maximum(z, jnp.log(1.0 + jnp.exp2(jnp.minimum(z, EXP2_MAX))) * LOG2_E)
        return sp, z - sp

    def carry_lanes(c0, c1):
        n = c0.shape[0]
        return jnp.concatenate([jnp.broadcast_to(c0, (n, tb)), jnp.broadcast_to(c1, (n, tb))], axis=1)

    def row_sums(sp):
        return jnp.sum(sp[:, :tb], axis=1, keepdims=True), jnp.sum(sp[:, tb:], axis=1, keepdims=True)

    def key_block(j):
        start = pl.multiple_of(j * tb, tb)
        return (_split_heads(k_ref[pl.ds(start, tb), :], tb),
                _split_heads(v_ref[pl.ds(start, tb), :], tb))

    def unfinished(c0, c1):
        return (jnp.min(jnp.minimum(c0, c1)) < SKIP_LOG2).astype(jnp.int32)

    qpad_ref[0:s, :] = q_ref[...]
    qpad_ref[s:, :] = jnp.zeros((pad, LANES), q_ref.dtype)

    def sweep(i, carry):
        (a_c0, a_c1, a_acc), (b_c0, b_c1, b_acc), worst = carry
        j = nblk - 1 - i
        row0 = pl.multiple_of(j * tb, tb)
        k2, v2 = key_block(j)
        q = qpad_ref[pl.ds(row0, 2 * tb + hb), :]
        z = lax.dot_general(q, k2, NT_DIMS, preferred_element_type=F32)
        sp0, lb0 = softplus2(jnp.where(before, z[0:tb], -MASKED))
        sp1, lb1 = softplus2(z[tb:2 * tb])
        sp2, lb2 = softplus2(z[2 * tb:])
        later = jnp.dot(jnp.concatenate([sp0, sp1, sp2], axis=0).astype(BF16), tri,
                        preferred_element_type=F32)
        w0 = jnp.exp2(lb0 + later[0:tb])
        w1 = jnp.exp2(lb1 + later[tb:2 * tb] - carry_lanes(a_c0, a_c1))
        w2 = jnp.exp2(lb2 + later[2 * tb:] - carry_lanes(b_c0, b_c1))
        pv = jnp.dot(jnp.concatenate([w0, w1, w2], axis=0).astype(BF16), v2, preferred_element_type=F32)
        s00, s01 = row_sums(sp0)
        s10, s11 = row_sums(sp1)
        s20, s21 = row_sums(sp2)
        a_c0, a_c1, a_acc = a_c0 + s10, a_c1 + s11, a_acc + pv[tb:2 * tb]
        b_c0, b_c1, b_acc = b_c0 + s20, b_c1 + s21, b_acc + pv[2 * tb:]
        out0 = pl.multiple_of(row0 + pad, hb)
        c0 = jnp.concatenate([a_c0[hb:], b_c0], axis=0)
        c1 = jnp.concatenate([a_c1[hb:], b_c1], axis=0)
        acc_ref[pl.ds(out0, tb), :] = jnp.concatenate([a_acc[hb:], b_acc], axis=0)
        c0_ref[pl.ds(out0, tb), :] = c0
        c1_ref[pl.ds(out0, tb), :] = c1
        worst = jnp.where(j >= 1, jnp.minimum(worst, jnp.minimum(c0, c1)), worst)
        return (s00, s01, pv[0:tb]), (a_c0[:hb], a_c1[:hb], a_acc[:hb]), worst

    def padding(n):
        return jnp.full((n, 1), MASKED, F32), jnp.full((n, 1), MASKED, F32), jnp.zeros((n, LANES), F32)

    init = (padding(tb), padding(hb), jnp.full((tb, 1), SKIP_LOG2, F32))
    a, b, worst = lax.fori_loop(0, nblk, sweep, init, unroll=SB_UNROLL if nblk % SB_UNROLL == 0 else 1)
    acc_ref[0:tb, :] = a[2]
    acc_ref[tb:pad, :] = b[2]
    o_ref[...] = acc_ref[0:s, :].astype(o_ref.dtype)

    @pl.when(unfinished(worst, worst) > 0)
    def _():
        def finish(hh, _):
            row0 = pl.multiple_of(hh * hb, hb)
            q = q_ref[pl.ds(row0, hb), :]

            def one_block(carry):
                j, _, c0, c1, acc = carry
                k2, v2 = key_block(j)
                z = lax.dot_general(q, k2, NT_DIMS, preferred_element_type=F32)
                sp, log_beta = softplus2(z)
                later = jnp.dot(sp.astype(BF16), tri, preferred_element_type=F32)
                w = jnp.exp2(log_beta + later - carry_lanes(c0, c1))
                acc = acc + jnp.dot(w.astype(BF16), v2, preferred_element_type=F32)
                s0, s1 = row_sums(sp)
                return j - 1, unfinished(c0 + s0, c1 + s1), c0 + s0, c1 + s1, acc

            first_unmet = hh // 2 - 3 + hh % 2
            c0, c1 = c0_ref[pl.ds(row0, hb), :], c1_ref[pl.ds(row0, hb), :]
            carry = (first_unmet, unfinished(c0, c1), c0, c1, acc_ref[pl.ds(row0, hb), :])
            carry = lax.while_loop(lambda c: (c[0] >= 0) & (c[1] > 0), one_block, carry)
            o_ref[pl.ds(row0, hb), :] = carry[4].astype(o_ref.dtype)
            return 0

        lax.fori_loop(3, 2 * nblk, finish, 0)


def _stick_breaking(q, kv, tb, to_cast):
    b, s, _ = q.shape
    pairs = SB_WIDTH // LANES
    cast_specs = _cast_rider_specs(to_cast, b * pairs, lambda i, p: i * pairs + p)
    pad = tb + tb // 2
    r = lax.broadcasted_iota(jnp.int32, (2 * tb, 2 * tb), 0)
    c = lax.broadcasted_iota(jnp.int32, (2 * tb, 2 * tb), 1)
    tri = -((r > c) & ((r < tb) == (c < tb))).astype(BF16)
    seq = lambda col: pl.BlockSpec((None, s, LANES), lambda i, p: (i, 0, col + p))
    return pl.pallas_call(
        functools.partial(_sb_kernel, tb=tb, n_cast=len(to_cast)),
        out_shape=(jax.ShapeDtypeStruct((b, s, SB_WIDTH), BF16),
                   *[jax.ShapeDtypeStruct(w.shape, BF16) for w in to_cast]),
        grid=(b, pairs),
        in_specs=[seq(0), seq(0), seq(pairs), _const_spec((2 * tb, 2 * tb)), *cast_specs],
        out_specs=(seq(0), *cast_specs),
        scratch_shapes=[pltpu.VMEM((s + pad, LANES), BF16), pltpu.VMEM((s + pad, LANES), F32),
                        pltpu.VMEM((s + pad, 1), F32), pltpu.VMEM((s + pad, 1), F32)],
        compiler_params=pltpu.CompilerParams(
            dimension_semantics=("arbitrary", "arbitrary"), vmem_limit_bytes=VMEM_LIMIT),
        name="stick_breaking",
    )(q, kv, kv, tri, *to_cast)


def _sb_layer_kernel(x_ref, sb_ref, qm_ref, k2_ref, v2_ref, wout_ref, gf_ref, wgu_ref, wd_ref, gn_ref, o_ref,
                     raw_ref, h_ref):
    mem_out = _memory_attention(qm_ref[...], k2_ref, v2_ref)
    o_ref[...] = (x_ref[...]
                  + jnp.dot(sb_ref[...], wout_ref[0:SB_WIDTH, :], preferred_element_type=F32)
                  + jnp.dot(mem_out.astype(BF16), wout_ref[SB_WIDTH:, :], preferred_element_type=F32))

    @pl.when(pl.program_id(1) >= 0)
    def _():
        _ffn_in_place(o_ref, raw_ref, h_ref, gf_ref, wgu_ref, wd_ref)
        o_ref[...] = _rms_scale(o_ref[...]) * gn_ref[...]


def _sb_layer(x, sb_out, q, k2, v2, layer, w_out, g_ffn, w_gu, w_down, g_final, tm):
    b, s, d = x.shape
    d_ff = w_down.shape[0]
    tile = lambda w, col=0: pl.BlockSpec((None, tm, w), lambda i, t: (i, t, col))
    return pl.pallas_call(
        _sb_layer_kernel,
        out_shape=jax.ShapeDtypeStruct((b, s, d), F32),
        grid=(b, s // tm),
        in_specs=[
            tile(d),
            tile(SB_WIDTH),
            tile(MEM_WIDTH, SB_WIDTH // MEM_WIDTH),
            _memory_kv_spec(k2, layer), _memory_kv_spec(v2, layer),
            _const_spec((MIX_WIDTH, d)),
            _const_spec((1, d)),
            _const_spec((d, 2 * d_ff)),
            _const_spec((d_ff, d)),
            _const_spec((1, d)),
        ],
        out_specs=tile(d),
        scratch_shapes=[pltpu.VMEM((tm, d), BF16)] * 2,
        compiler_params=pltpu.CompilerParams(
            dimension_semantics=("arbitrary", "arbitrary"), vmem_limit_bytes=VMEM_LIMIT),
        name="sb_layer",
    )(x, sb_out, q, k2, v2, w_out, g_ffn.reshape(1, d), w_gu, w_down, g_final.reshape(1, d))


def _block_diag(w_group):
    g, c, _ = w_group.shape
    eye = jnp.eye(g, dtype=w_group.dtype)
    return jnp.einsum("gcd,gh->gchd", w_group, eye).reshape(g * c, g * c)


def kernel(x, mem, mem_norm, a_norm_mix, a_w_in, a_w_group, a_scale, a_w_mem_kv, a_w_out, a_norm_ffn,
           a_w_gu, a_w_down, kv_norm, w_kv, b_norm_mix, b_w_q, b_w_mem_kv, b_w_out, b_norm_ffn, b_w_gu,
           b_w_down, final_norm):
    b, s, d = x.shape
    na, nb = a_w_in.shape[0], b_w_q.shape[0]
    assert nb == 1, "exactly one stick-breaking layer"
    tm, tm_sb = min(ROW_TILE, s), min(SB_LAYER_ROW_TILE, s)
    assert s % tm == 0 and s % tm_sb == 0 and s % SB_TILE == 0, "sequence length must be a multiple of the row tiles"

    assert na == 1, "one pooling layer: its weight preparation also hosts every layer's memory K/V"
    i = 0
    w_mem_kv = jnp.concatenate([a_w_mem_kv, b_w_mem_kv], axis=0).astype(BF16)
    w_in, k2, v2, w_out, w_gu, w_down, w_kv16, w_q16 = _prepare_weights(
        a_w_in[i], _block_diag(a_w_group[i]), a_scale[i], mem, mem_norm, w_mem_kv,
        (a_w_out[i], a_w_gu[i], a_w_down[i], w_kv, b_w_q[0]))
    x, kv, q = _pool_layer(x, a_norm_mix[i], w_in, k2, v2, i, w_out, a_norm_ffn[i], w_gu, w_down, tm,
                           qkv=(kv_norm, w_kv16, b_norm_mix[0], w_q16))

    sb_out, w_out, w_gu, w_down = _stick_breaking(q, kv, SB_TILE, (b_w_out[0], b_w_gu[0], b_w_down[0]))
    return _sb_layer(x, sb_out, q, k2, v2, na, w_out, b_norm_ffn[0], w_gu, w_down, final_norm, tm_sb)
```

```python
import functools

import jax
import jax.numpy as jnp
from jax import lax
from jax.experimental import pallas as pl
from jax.experimental.pallas import tpu as pltpu

HEAD_DIM = 64
N_SB_HEADS = 12
SB_WIDTH = N_SB_HEADS * HEAD_DIM
N_MEM_HEADS = 4
MEM_WIDTH = N_MEM_HEADS * HEAD_DIM
MIX_WIDTH = SB_WIDTH + MEM_WIDTH
POOL_WINDOWS = (2, 4, 8, 16)
POOL_WIDTH = SB_WIDTH
POOL_GROUP = POOL_WIDTH // len(POOL_WINDOWS)
EPS = 1e-6
LOG2_E = 1.4426950408889634
QK_SCALE_LOG2 = HEAD_DIM ** -0.5 * LOG2_E

LANES = 128
ROW_TILE = 512
SB_LAYER_ROW_TILE = 1024
BF16_ROWS = 16
FOLD_ROWS = 128
POOL_PAD = 32
FF_CHUNK = 256
SB_TILE = 128
SB_UNROLL = 32
SKIP_LOG2 = 151.0
EXP2_MAX = 126.0
MASKED = 1e4
VMEM_LIMIT = 56 * 1024 * 1024

F32 = jnp.float32
BF16 = jnp.bfloat16
NT_DIMS = (((1,), (1,)), ((), ()))


def _const_spec(shape):
    nd = len(shape)
    return pl.BlockSpec(shape, lambda *_: (0,) * nd, pipeline_mode=pl.Buffered(1))


def _inv_rms(x):
    return lax.rsqrt(jnp.mean(x * x, axis=-1, keepdims=True) + EPS)


def _rms_scale(x):
    return x * _inv_rms(x)


def _split_heads(t, rows):
    lane = lax.broadcasted_iota(jnp.int32, (rows, LANES), 1)
    first = lane < HEAD_DIM
    zero = jnp.zeros_like(t)
    return jnp.concatenate([jnp.where(first, t, zero), jnp.where(first, zero, t)], axis=0)


def _row_blocks(rows, steps):
    for rb in range(BF16_ROWS, rows + 1, BF16_ROWS):
        if rows % rb == 0 and rows // rb <= steps:
            return rb
    raise ValueError(f"cannot split {rows} rows over {steps} steps")


def _cast_rider_specs(to_cast, steps, step_of):
    specs = []
    for w in to_cast:
        rb = _row_blocks(w.shape[0], steps)
        last = w.shape[0] // rb - 1
        specs.append(pl.BlockSpec((rb, w.shape[1]), lambda *g, last=last: (jnp.minimum(step_of(*g), last), 0)))
    return specs


def _cast_riders(cast_in, cast_out):
    for src, dst in zip(cast_in, cast_out):
        dst[...] = src[...].astype(dst.dtype)


def _memory_kv_step(mem_ref, g_ref, w_ref, k2_ref, v2_ref):
    m = mem_ref.shape[0]
    h = (_rms_scale(mem_ref[...]) * g_ref[...]).astype(BF16)
    for layer in range(w_ref.shape[0]):
        kv = jnp.dot(h, w_ref[layer], preferred_element_type=F32)
        for p in range(MEM_WIDTH // LANES):
            k2_ref[layer, p] = _split_heads(kv[:, p * LANES:(p + 1) * LANES], m).astype(BF16)
            v2_ref[layer, p] = _split_heads(kv[:, MEM_WIDTH + p * LANES:MEM_WIDTH + (p + 1) * LANES], m).astype(BF16)


def _memory_kv_spec(k2, layer):
    return pl.BlockSpec((None, None) + k2.shape[2:], lambda i, t: (i, layer, 0, 0, 0))


def _memory_attention(qm, k2_ref, v2_ref):
    t = qm.shape[0]
    m = k2_ref.shape[1] // 2
    lane = lax.broadcasted_iota(jnp.int32, (t, LANES), 1)
    outs = []
    for p in range(MEM_WIDTH // LANES):
        logits = lax.dot_general(qm[:, p * LANES:(p + 1) * LANES], k2_ref[p], NT_DIMS,
                                 preferred_element_type=F32)
        probs, inv = [], []
        for hh in range(2):
            lg = logits[:, hh * m:(hh + 1) * m]
            pe = jnp.exp2(lg - jnp.max(lg, axis=1, keepdims=True))
            probs.append(pe)
            inv.append(1.0 / jnp.sum(pe, axis=1, keepdims=True))
        pv = jnp.dot(jnp.concatenate(probs, axis=1).astype(BF16), v2_ref[p],
                     preferred_element_type=F32)
        outs.append(pv * jnp.where(lane < HEAD_DIM, inv[0], inv[1]))
    return jnp.concatenate(outs, axis=1)


def _ffn_weight_copies(wgu_hbm, wd_hbm, wgu_ref, wd_ref, sem):
    return (pltpu.make_async_copy(wgu_hbm, wgu_ref, sem.at[0]), pltpu.make_async_copy(wd_hbm, wd_ref, sem.at[1]))


def _first_step():
    return (pl.program_id(0) == 0) & (pl.program_id(1) == 0)


def _ffn_in_place(o_ref, raw_ref, h_ref, g_ref, wgu_ref, wd_ref):
    xg = o_ref[...] * g_ref[...]
    inv_rms = _inv_rms(o_ref[...])
    raw_ref[...] = xg.astype(BF16)
    h_ref[...] = (xg * inv_rms).astype(BF16)
    d_ff = wd_ref.shape[0]
    for c in range(d_ff // FF_CHUNK):
        lo, hi = c * FF_CHUNK, (c + 1) * FF_CHUNK
        h = raw_ref[...] if c == 0 else h_ref[...]
        gate = jnp.dot(h, wgu_ref[:, lo:hi], preferred_element_type=F32)
        up = jnp.dot(h, wgu_ref[:, d_ff + lo:d_ff + hi], preferred_element_type=F32)
        if c == 0:
            gate, up = gate * inv_rms, up * inv_rms
        act = gate * (1.0 / (1.0 + jnp.exp(-gate))) * up
        o_ref[...] += jnp.dot(act.astype(BF16), wd_ref[lo:hi, :], preferred_element_type=F32)


def _fold_kernel(win_ref, wgrp_ref, scale_ref, mem_ref, gmem_ref, wmem_ref, *rest, n_cast):
    o_ref, k2_ref, v2_ref = rest[n_cast:n_cast + 3]
    _cast_riders(rest[:n_cast], rest[n_cast + 3:])
    _memory_kv_step(mem_ref, gmem_ref, wmem_ref, k2_ref, v2_ref)
    a = win_ref[:, :POOL_WIDTH]
    b = wgrp_ref[...]
    a_hi, b_hi = a.astype(BF16), b.astype(BF16)
    a_lo, b_lo = (a - a_hi.astype(F32)).astype(BF16), (b - b_hi.astype(F32)).astype(BF16)
    prod = (jnp.dot(a_hi, b_hi, preferred_element_type=F32) + jnp.dot(a_hi, b_lo, preferred_element_type=F32)
            + jnp.dot(a_lo, b_hi, preferred_element_type=F32))
    o_ref[:, :POOL_WIDTH] = (prod * scale_ref[...]).astype(BF16)
    o_ref[:, POOL_WIDTH:] = win_ref[:, POOL_WIDTH:].astype(BF16)


def _prepare_weights(w_in, w_group_full, scale, mem, mem_norm, w_mem_kv, to_cast):
    d = w_in.shape[0]
    b, m, _ = mem.shape
    n_layers = w_mem_kv.shape[0]
    pairs = MEM_WIDTH // LANES
    tr = min(FOLD_ROWS, d)
    steps = d // tr
    assert steps >= b, "one memory batch element per grid step"
    rider_specs = _cast_rider_specs(to_cast, steps, lambda i: i)
    folded_spec = pl.BlockSpec((tr, MIX_WIDTH), lambda i: (i, 0))
    batch = lambda i: jnp.minimum(i, b - 1)
    kv_shape = jax.ShapeDtypeStruct((b, n_layers, pairs, 2 * m, LANES), BF16)
    kv_spec = pl.BlockSpec((None, n_layers, pairs, 2 * m, LANES), lambda i: (batch(i), 0, 0, 0, 0))
    return pl.pallas_call(
        functools.partial(_fold_kernel, n_cast=len(to_cast)),
        out_shape=(jax.ShapeDtypeStruct((d, MIX_WIDTH), BF16), kv_shape, kv_shape,
                   *[jax.ShapeDtypeStruct(w.shape, BF16) for w in to_cast]),
        grid=(steps,),
        in_specs=[folded_spec, _const_spec((POOL_WIDTH, POOL_WIDTH)), _const_spec((1, POOL_WIDTH)),
                  pl.BlockSpec((None, m, d), lambda i: (batch(i), 0, 0)), _const_spec((1, d)),
                  _const_spec((n_layers, d, 2 * MEM_WIDTH)), *rider_specs],
        out_specs=(folded_spec, kv_spec, kv_spec, *rider_specs),
        compiler_params=pltpu.CompilerParams(dimension_semantics=("arbitrary",), vmem_limit_bytes=VMEM_LIMIT),
        name="prepare_weights",
    )(w_in, w_group_full, scale.reshape(1, POOL_WIDTH), mem, mem_norm.reshape(1, d), w_mem_kv, *to_cast)


def _qkv_from(x, gkv_ref, gq_ref, wkv_ref, wq_ref, kv_ref, q_ref):
    inv_rms = _inv_rms(x)
    kv = jnp.dot((x * gkv_ref[...]).astype(BF16), wkv_ref[...], preferred_element_type=F32)
    kv_ref[...] = (kv * inv_rms).astype(BF16)
    q = jnp.dot((x * gq_ref[...]).astype(BF16), wq_ref[...], preferred_element_type=F32)
    q_ref[...] = (q * (inv_rms * QK_SCALE_LOG2)).astype(BF16)


def _pool_layer_kernel(x_ref, g_ref, win_ref, k2_ref, v2_ref, wout_ref, gf_ref, wgu_ref, wd_ref, *rest,
                       tm, with_qkv):
    if with_qkv:
        (gkv_ref, gq_ref, wkv_ref, wq_ref, o_ref, kv_ref, q_ref, e_ref, s2_ref, s4_ref, s8_ref, raw_ref, h_ref,
         wgu_vmem, wd_vmem, sem) = rest
    else:
        o_ref, e_ref, s2_ref, s4_ref, s8_ref, raw_ref, h_ref, wgu_vmem, wd_vmem, sem = rest
    t = pl.program_id(1)
    copies = _ffn_weight_copies(wgu_ref, wd_ref, wgu_vmem, wd_vmem, sem)

    @pl.when(_first_step())
    def _():
        for c in copies:
            c.start()

    x = x_ref[...]
    proj = jnp.dot((x * g_ref[...]).astype(BF16), win_ref[...], preferred_element_type=F32) * _inv_rms(x)
    u = proj[:, :POOL_WIDTH]

    @pl.when(t == 0)
    def _():
        e_ref[0:POOL_PAD, :] = jnp.zeros((POOL_PAD, POOL_WIDTH), F32)

    @pl.when(t > 0)
    def _():
        e_ref[POOL_PAD - 16:POOL_PAD, :] = e_ref[tm + POOL_PAD - 16:tm + POOL_PAD, :]

    e_ref[POOL_PAD:, :] = u
    n = tm + POOL_PAD
    s2_ref[8:, :] = e_ref[8:, :] + e_ref[7:n - 1, :]
    s4_ref[16:, 128:] = s2_ref[16:, 128:] + s2_ref[14:n - 2, 128:]
    s8_ref[24:, 384:] = s4_ref[24:, 384:] + s4_ref[20:n - 4, 384:]
    s16 = s8_ref[32:, 512:] + s8_ref[24:n - 8, 512:]

    pos1 = t * tm + lax.broadcasted_iota(jnp.int32, (tm, 1), 0) + 1
    inv = [1.0 / jnp.minimum(pos1, w).astype(F32) for w in POOL_WINDOWS]
    lane = lax.broadcasted_iota(jnp.int32, (tm, LANES), 1)
    low = lane < (POOL_GROUP - LANES)
    grouped = jnp.concatenate([
        s2_ref[POOL_PAD:, 0:128] * inv[0],
        jnp.where(low, s2_ref[POOL_PAD:, 128:256] * inv[0], s4_ref[POOL_PAD:, 128:256] * inv[1]),
        s4_ref[POOL_PAD:, 256:384] * inv[1],
        s8_ref[POOL_PAD:, 384:512] * inv[2],
        jnp.where(low, s8_ref[POOL_PAD:, 512:640] * inv[2], s16[:, 0:128] * inv[3]),
        s16[:, 128:256] * inv[3],
    ], axis=1) - u

    qm = (proj[:, POOL_WIDTH:] * QK_SCALE_LOG2).astype(BF16)
    mem_out = _memory_attention(qm, k2_ref, v2_ref)
    o_ref[...] = (x
                  + jnp.dot(grouped.astype(BF16), wout_ref[0:POOL_WIDTH, :], preferred_element_type=F32)
                  + jnp.dot(mem_out.astype(BF16), wout_ref[POOL_WIDTH:, :], preferred_element_type=F32))

    @pl.when(_first_step())
    def _():
        for c in copies:
            c.wait()

    @pl.when(t >= 0)
    def _():
        _ffn_in_place(o_ref, raw_ref, h_ref, gf_ref, wgu_vmem, wd_vmem)

    if with_qkv:
        @pl.when(t >= 0)
        def _():
            _qkv_from(o_ref[...], gkv_ref, gq_ref, wkv_ref, wq_ref, kv_ref, q_ref)


def _pool_layer(x, g, w_in_folded, k2, v2, layer, w_out, g_ffn, w_gu, w_down, tm, qkv=None):
    b, s, d = x.shape
    d_ff = w_down.shape[0]
    tile = lambda w: pl.BlockSpec((None, tm, w), lambda i, t: (i, t, 0))
    out_shape, out_specs = jax.ShapeDtypeStruct((b, s, d), F32), tile(d)
    extra_in, extra_args = [], []
    if qkv is not None:
        g_kv, w_kv, g_q, w_q = qkv
        extra_in = [_const_spec((1, d)), _const_spec((1, d)),
                    _const_spec((d, 2 * SB_WIDTH)), _const_spec((d, MIX_WIDTH))]
        extra_args = [g_kv.reshape(1, d), g_q.reshape(1, d), w_kv, w_q]
        out_shape = (out_shape, jax.ShapeDtypeStruct((b, s, 2 * SB_WIDTH), BF16),
                     jax.ShapeDtypeStruct((b, s, MIX_WIDTH), BF16))
        out_specs = (out_specs, tile(2 * SB_WIDTH), tile(MIX_WIDTH))
    return pl.pallas_call(
        functools.partial(_pool_layer_kernel, tm=tm, with_qkv=qkv is not None),
        out_shape=out_shape,
        grid=(b, s // tm),
        in_specs=[
            tile(d),
            _const_spec((1, d)),
            _const_spec((d, MIX_WIDTH)),
            _memory_kv_spec(k2, layer), _memory_kv_spec(v2, layer),
            _const_spec((MIX_WIDTH, d)),
            _const_spec((1, d)),
            pl.BlockSpec(memory_space=pl.ANY),
            pl.BlockSpec(memory_space=pl.ANY),
        ] + extra_in,
        out_specs=out_specs,
        scratch_shapes=[pltpu.VMEM((tm + POOL_PAD, POOL_WIDTH), F32)] * 4 + [pltpu.VMEM((tm, d), BF16)] * 2
        + [pltpu.VMEM((d, 2 * d_ff), BF16), pltpu.VMEM((d_ff, d), BF16), pltpu.SemaphoreType.DMA((2,))],
        compiler_params=pltpu.CompilerParams(
            dimension_semantics=("arbitrary", "arbitrary"), vmem_limit_bytes=VMEM_LIMIT),
        name="pool_layer_qkv" if qkv is not None else "pool_layer",
    )(x, g.reshape(1, d), w_in_folded, k2, v2, w_out, g_ffn.reshape(1, d), w_gu, w_down, *extra_args)


def _sb_kernel(q_ref, k_ref, v_ref, tri_ref, *rest, tb, n_cast):
    cast_in, (o_ref, *cast_out), scratch = rest[:n_cast], rest[n_cast:2 * n_cast + 1], rest[2 * n_cast + 1:]
    qpad_ref, acc_ref, c0_ref, c1_ref = scratch
    _cast_riders(cast_in, cast_out)

    s = q_ref.shape[0]
    nblk = s // tb
    hb = tb // 2
    pad = tb + hb
    tri = tri_ref[...]
    rows = lax.broadcasted_iota(jnp.int32, (tb, 2 * tb), 0)
    cols = lax.broadcasted_iota(jnp.int32, (tb, 2 * tb), 1)
    before = jnp.where(cols < tb, cols, cols - tb) < rows

    def softplus2(z):
        sp = jnp.maximum(z, jnp.log(1.0 + jnp.exp2(jnp.minimum(z, EXP2_MAX))) * LOG2_E)
        return sp, z - sp

    def carry_lanes(c0, c1):
        n = c0.shape[0]
        return jnp.concatenate([jnp.broadcast_to(c0, (n, tb)), jnp.broadcast_to(c1, (n, tb))], axis=1)

    def row_sums(sp):
        return jnp.sum(sp[:, :tb], axis=1, keepdims=True), jnp.sum(sp[:, tb:], axis=1, keepdims=True)

    def key_block(j):
        start = pl.multiple_of(j * tb, tb)
        return (_split_heads(k_ref[pl.ds(start, tb), :], tb),
                _split_heads(v_ref[pl.ds(start, tb), :], tb))

    def unfinished(c0, c1):
        return (jnp.min(jnp.minimum(c0, c1)) < SKIP_LOG2).astype(jnp.int32)

    qpad_ref[0:s, :] = q_ref[...]
    qpad_ref[s:, :] = jnp.zeros((pad, LANES), q_ref.dtype)

    def sweep(i, carry):
        (a_c0, a_c1, a_acc), (b_c0, b_c1, b_acc), worst = carry
        j = nblk - 1 - i
        row0 = pl.multiple_of(j * tb, tb)
        k2, v2 = key_block(j)
        q = qpad_ref[pl.ds(row0, 2 * tb + hb), :]
        z = lax.dot_general(q, k2, NT_DIMS, preferred_element_type=F32)
        sp0, lb0 = softplus2(jnp.where(before, z[0:tb], -MASKED))
        sp1, lb1 = softplus2(z[tb:2 * tb])
        sp2, lb2 = softplus2(z[2 * tb:])
        later = jnp.dot(jnp.concatenate([sp0, sp1, sp2], axis=0).astype(BF16), tri,
                        preferred_element_type=F32)
        w0 = jnp.exp2(lb0 + later[0:tb])
        w1 = jnp.exp2(lb1 + later[tb:2 * tb] - carry_lanes(a_c0, a_c1))
        w2 = jnp.exp2(lb2 + later[2 * tb:] - carry_lanes(b_c0, b_c1))
        pv = jnp.dot(jnp.concatenate([w0, w1, w2], axis=0).astype(BF16), v2, preferred_element_type=F32)
        s00, s01 = row_sums(sp0)
        s10, s11 = row_sums(sp1)
        s20, s21 = row_sums(sp2)
        a_c0, a_c1, a_acc = a_c0 + s10, a_c1 + s11, a_acc + pv[tb:2 * tb]
        b_c0, b_c1, b_acc = b_c0 + s20, b_c1 + s21, b_acc + pv[2 * tb:]
        out0 = pl.multiple_of(row0 + pad, hb)
        c0 = jnp.concatenate([a_c0[hb:], b_c0], axis=0)
        c1 = jnp.concatenate([a_c1[hb:], b_c1], axis=0)
        acc_ref[pl.ds(out0, tb), :] = jnp.concatenate([a_acc[hb:], b_acc], axis=0)
        c0_ref[pl.ds(out0, tb), :] = c0
        c1_ref[pl.ds(out0, tb), :] = c1
        worst = jnp.where(j >= 1, jnp.minimum(worst, jnp.minimum(c0, c1)), worst)
        return (s00, s01, pv[0:tb]), (a_c0[:hb], a_c1[:hb], a_acc[:hb]), worst

    def padding(n):
        return jnp.full((n, 1), MASKED, F32), jnp.full((n, 1), MASKED, F32), jnp.zeros((n, LANES), F32)

    init = (padding(tb), padding(hb), jnp.full((tb, 1), SKIP_LOG2, F32))
    a, b, worst = lax.fori_loop(0, nblk, sweep, init, unroll=SB_UNROLL if nblk % SB_UNROLL == 0 else 1)
    acc_ref[0:tb, :] = a[2]
    acc_ref[tb:pad, :] = b[2]
    o_ref[...] = acc_ref[0:s, :].astype(o_ref.dtype)

    @pl.when(unfinished(worst, worst) > 0)
    def _():
        def finish(hh, _):
            row0 = pl.multiple_of(hh * hb, hb)
            q = q_ref[pl.ds(row0, hb), :]

            def one_block(carry):
                j, _, c0, c1, acc = carry
                k2, v2 = key_block(j)
                z = lax.dot_general(q, k2, NT_DIMS, preferred_element_type=F32)
                sp, log_beta = softplus2(z)
                later = jnp.dot(sp.astype(BF16), tri, preferred_element_type=F32)
                w = jnp.exp2(log_beta + later - carry_lanes(c0, c1))
                acc = acc + jnp.dot(w.astype(BF16), v2, preferred_element_type=F32)
                s0, s1 = row_sums(sp)
                return j - 1, unfinished(c0 + s0, c1 + s1), c0 + s0, c1 + s1, acc

            first_unmet = hh // 2 - 3 + hh % 2
            c0, c1 = c0_ref[pl.ds(row0, hb), :], c1_ref[pl.ds(row0, hb), :]
            carry = (first_unmet, unfinished(c0, c1), c0, c1, acc_ref[pl.ds(row0, hb), :])
            carry = lax.while_loop(lambda c: (c[0] >= 0) & (c[1] > 0), one_block, carry)
            o_ref[pl.ds(row0, hb), :] = carry[4].astype(o_ref.dtype)
            return 0

        lax.fori_loop(3, 2 * nblk, finish, 0)


def _stick_breaking(q, kv, tb, to_cast):
    b, s, _ = q.shape
    pairs = SB_WIDTH // LANES
    cast_specs = _cast_rider_specs(to_cast, b * pairs, lambda i, p: i * pairs + p)
    pad = tb + tb // 2
    r = lax.broadcasted_iota(jnp.int32, (2 * tb, 2 * tb), 0)
    c = lax.broadcasted_iota(jnp.int32, (2 * tb, 2 * tb), 1)
    tri = -((r > c) & ((r < tb) == (c < tb))).astype(BF16)
    seq = lambda col: pl.BlockSpec((None, s, LANES), lambda i, p: (i, 0, col + p))
    return pl.pallas_call(
        functools.partial(_sb_kernel, tb=tb, n_cast=len(to_cast)),
        out_shape=(jax.ShapeDtypeStruct((b, s, SB_WIDTH), BF16),
                   *[jax.ShapeDtypeStruct(w.shape, BF16) for w in to_cast]),
        grid=(b, pairs),
        in_specs=[seq(0), seq(0), seq(pairs), _const_spec((2 * tb, 2 * tb)), *cast_specs],
        out_specs=(seq(0), *cast_specs),
        scratch_shapes=[pltpu.VMEM((s + pad, LANES), BF16), pltpu.VMEM((s + pad, LANES), F32),
                        pltpu.VMEM((s + pad, 1), F32), pltpu.VMEM((s + pad, 1), F32)],
        compiler_params=pltpu.CompilerParams(
            dimension_semantics=("arbitrary", "arbitrary"), vmem_limit_bytes=VMEM_LIMIT),
        name="stick_breaking",
    )(q, kv, kv, tri, *to_cast)


def _sb_layer_kernel(x_ref, sb_ref, qm_ref, k2_ref, v2_ref, wout_ref, gf_ref, wgu_ref, wd_ref, gn_ref, o_ref,
                     raw_ref, h_ref, wgu_vmem, wd_vmem, sem):
    copies = _ffn_weight_copies(wgu_ref, wd_ref, wgu_vmem, wd_vmem, sem)

    @pl.when(_first_step())
    def _():
        for c in copies:
            c.start()

    mem_out = _memory_attention(qm_ref[...], k2_ref, v2_ref)
    o_ref[...] = (x_ref[...]
                  + jnp.dot(sb_ref[...], wout_ref[0:SB_WIDTH, :], preferred_element_type=F32)
                  + jnp.dot(mem_out.astype(BF16), wout_ref[SB_WIDTH:, :], preferred_element_type=F32))

    @pl.when(_first_step())
    def _():
        for c in copies:
            c.wait()

    @pl.when(pl.program_id(1) >= 0)
    def _():
        _ffn_in_place(o_ref, raw_ref, h_ref, gf_ref, wgu_vmem, wd_vmem)
        o_ref[...] = _rms_scale(o_ref[...]) * gn_ref[...]


def _sb_layer(x, sb_out, q, k2, v2, layer, w_out, g_ffn, w_gu, w_down, g_final, tm):
    b, s, d = x.shape
    d_ff = w_down.shape[0]
    tile = lambda w, col=0: pl.BlockSpec((None, tm, w), lambda i, t: (i, t, col))
    return pl.pallas_call(
        _sb_layer_kernel,
        out_shape=jax.ShapeDtypeStruct((b, s, d), F32),
        grid=(b, s // tm),
        in_specs=[
            tile(d),
            tile(SB_WIDTH),
            tile(MEM_WIDTH, SB_WIDTH // MEM_WIDTH),
            _memory_kv_spec(k2, layer), _memory_kv_spec(v2, layer),
            _const_spec((MIX_WIDTH, d)),
            _const_spec((1, d)),
            pl.BlockSpec(memory_space=pl.ANY),
            pl.BlockSpec(memory_space=pl.ANY),
            _const_spec((1, d)),
        ],
        out_specs=tile(d),
        scratch_shapes=[pltpu.VMEM((tm, d), BF16)] * 2
        + [pltpu.VMEM((d, 2 * d_ff), BF16), pltpu.VMEM((d_ff, d), BF16), pltpu.SemaphoreType.DMA((2,))],
        compiler_params=pltpu.CompilerParams(
            dimension_semantics=("arbitrary", "arbitrary"), vmem_limit_bytes=VMEM_LIMIT),
        name="sb_layer",
    )(x, sb_out, q, k2, v2, w_out, g_ffn.reshape(1, d), w_gu, w_down, g_final.reshape(1, d))


def _block_diag(w_group):
    g, c, _ = w_group.shape
    eye = jnp.eye(g, dtype=w_group.dtype)
    return jnp.einsum("gcd,gh->gchd", w_group, eye).reshape(g * c, g * c)


def kernel(x, mem, mem_norm, a_norm_mix, a_w_in, a_w_group, a_scale, a_w_mem_kv, a_w_out, a_norm_ffn,
           a_w_gu, a_w_down, kv_norm, w_kv, b_norm_mix, b_w_q, b_w_mem_kv, b_w_out, b_norm_ffn, b_w_gu,
           b_w_down, final_norm):
    b, s, d = x.shape
    na, nb = a_w_in.shape[0], b_w_q.shape[0]
    assert nb == 1, "exactly one stick-breaking layer"
    tm, tm_sb = min(ROW_TILE, s), min(SB_LAYER_ROW_TILE, s)
    assert s % tm == 0 and s % tm_sb == 0 and s % SB_TILE == 0, "sequence length must be a multiple of the row tiles"

    assert na == 1, "one pooling layer: its weight preparation also hosts every layer's memory K/V"
    i = 0
    w_mem_kv = jnp.concatenate([a_w_mem_kv, b_w_mem_kv], axis=0).astype(BF16)
    w_in, k2, v2, w_out, w_gu, w_down, w_kv16, w_q16 = _prepare_weights(
        a_w_in[i], _block_diag(a_w_group[i]), a_scale[i], mem, mem_norm, w_mem_kv,
        (a_w_out[i], a_w_gu[i], a_w_down[i], w_kv, b_w_q[0]))
    x, kv, q = _pool_layer(x, a_norm_mix[i], w_in, k2, v2, i, w_out, a_norm_ffn[i], w_gu, w_down, tm,
                           qkv=(kv_norm, w_kv16, b_norm_mix[0], w_q16))

    sb_out, w_out, w_gu, w_down = _stick_breaking(q, kv, SB_TILE, (b_w_out[0], b_w_gu[0], b_w_down[0]))
    return _sb_layer(x, sb_out, q, k2, v2, na, w_out, b_norm_ffn[0], w_gu, w_down, final_norm, tm_sb)
```
